```python
import jax, jax.numpy as jnp
from jax import lax
import numpy as np

D_MODEL = 1024
BATCH = 8
SEQ = 4096
DEPTH = 1

CHUNK = 64
MIX_WIDTH = D_MODEL
ATTN_WIDTH = MIX_WIDTH // 2
N_HEADS = 8
HEAD_DIM = ATTN_WIDTH // N_HEADS
POOL_WIDTH = MIX_WIDTH - ATTN_WIDTH
POOL_WINDOWS = (2, 4, 8, 16)
N_POOL_GROUPS = len(POOL_WINDOWS)
POOL_GROUP = POOL_WIDTH // N_POOL_GROUPS
IN_WIDTH = 3 * ATTN_WIDTH + N_HEADS + POOL_WIDTH
D_FF = -(-(8 * D_MODEL) // (3 * 256)) * 256
Q_BLOCK = 128
EPS = 1e-6

kernel_name = "fox_pool_hybrid_block"


def rmsnorm(x, g):
    xf = x.astype(jnp.float32)
    y = xf * lax.rsqrt(jnp.mean(xf * xf, axis=-1, keepdims=True) + EPS)
    return (y * g.astype(jnp.float32)).astype(x.dtype)


def forgetting_attention(q, k, v, log_f):
    S, Dh = q.shape[2], q.shape[3]
    c = jnp.cumsum(log_f, axis=-1)
    qf = q.astype(jnp.float32) * (Dh ** -0.5)
    kf = k.astype(jnp.float32)
    vf = v.astype(jnp.float32)
    outs = []
    for i in range(S // Q_BLOCK):
        q0, q1 = i * Q_BLOCK, (i + 1) * Q_BLOCK
        logits = jnp.einsum('bhqd,bhkd->bhqk', qf[:, :, q0:q1], kf[:, :, :q1])
        logits = logits + c[:, :, q0:q1, None] - c[:, :, None, :q1]
        t_pos = jnp.arange(q0, q1)[:, None]
        s_pos = jnp.arange(q1)[None, :]
        logits = jnp.where(s_pos <= t_pos, logits, -jnp.inf)
        p = jax.nn.softmax(logits, axis=-1)
        outs.append(jnp.einsum('bhqk,bhkd->bhqd', p, vf[:, :, :q1]))
    return jnp.concatenate(outs, axis=2).astype(q.dtype)


def multiscale_pool(u, w_pool, pool_scale):
    B, S, C = u.shape
    uf = u.astype(jnp.float32)
    cs = jnp.concatenate([jnp.zeros((B, 1, C), jnp.float32), jnp.cumsum(uf, axis=1)], axis=1)
    t = jnp.arange(S)
    groups = []
    for g, w in enumerate(POOL_WINDOWS):
        lo, hi = g * POOL_GROUP, (g + 1) * POOL_GROUP
        start = jnp.maximum(t + 1 - w, 0)
        csg = cs[:, :, lo:hi]
        window_sum = csg[:, 1:] - csg[:, start]
        count = (t + 1 - start).astype(jnp.float32)[None, :, None]
        groups.append(window_sum / count - uf[:, :, lo:hi])
    pooled = jnp.stack(groups, axis=2)
    mixed = jnp.einsum('bsgc,gcd->bsgd', pooled, w_pool.astype(jnp.float32)).reshape(B, S, C)
    return (mixed * pool_scale.astype(jnp.float32)).astype(u.dtype)


def setup_inputs(seed: int = 0) -> dict:
    key = jax.random.key(seed)
    ks = jax.random.split(key, 13)
    f32 = jnp.float32
    nrm = lambda k, shape, scale: jax.random.normal(k, shape, f32) * scale
    return {
        "x": jax.random.normal(ks[0], (BATCH, SEQ, D_MODEL), f32),
        "norm1_g": 1.0 + nrm(ks[1], (DEPTH, D_MODEL), 0.05),
        "w_in": nrm(ks[2], (DEPTH, D_MODEL, IN_WIDTH), D_MODEL ** -0.5),
        "b_forget": 2.0 + nrm(ks[3], (DEPTH, N_HEADS), 0.5),
        "w_pool": nrm(ks[4], (DEPTH, N_POOL_GROUPS, POOL_GROUP, POOL_GROUP), POOL_GROUP ** -0.5),
        "pool_scale": 1.0 + nrm(ks[5], (DEPTH, POOL_WIDTH), 0.1),
        "w_out": nrm(ks[6], (DEPTH, MIX_WIDTH, D_MODEL), MIX_WIDTH ** -0.5),
        "norm2_g": 1.0 + nrm(ks[7], (DEPTH, D_MODEL), 0.05),
        "w_gate": nrm(ks[8], (DEPTH, D_MODEL, D_FF), D_MODEL ** -0.5),
        "w_up": nrm(ks[9], (DEPTH, D_MODEL, D_FF), D_MODEL ** -0.5),
        "w_down": nrm(ks[10], (DEPTH, D_FF, D_MODEL), D_FF ** -0.5),
        "final_g": 1.0 + nrm(ks[11], (D_MODEL,), 0.05),
    }


def reference(x, norm1_g, w_in, b_forget, w_pool, pool_scale, w_out, norm2_g, w_gate, w_up, w_down, final_g):
    B, S, _ = x.shape
    for layer in range(DEPTH):
        h = rmsnorm(x, norm1_g[layer])
        proj = jnp.einsum('bsd,de->bse', h, w_in[layer])
        a0 = ATTN_WIDTH
        q = proj[..., 0:a0]
        k = proj[..., a0:2 * a0]
        v = proj[..., 2 * a0:3 * a0]
        f_logit = proj[..., 3 * a0:3 * a0 + N_HEADS]
        u = proj[..., 3 * a0 + N_HEADS:]
        to_heads = lambda t: t.reshape(B, S, N_HEADS, HEAD_DIM).transpose(0, 2, 1, 3)
        log_f = jax.nn.log_sigmoid(f_logit.astype(jnp.float32) + b_forget[layer].astype(jnp.float32))
        log_f = log_f.transpose(0, 2, 1)
        attn = forgetting_attention(to_heads(q), to_heads(k), to_heads(v), log_f)
        attn = attn.transpose(0, 2, 1, 3).reshape(B, S, ATTN_WIDTH)
        pool = multiscale_pool(u, w_pool[layer], pool_scale[layer])
        mixed = jnp.concatenate([attn, pool], axis=-1)
        x = x + jnp.einsum('bse,ed->bsd', mixed, w_out[layer])
        h2 = rmsnorm(x, norm2_g[layer])
        gate = jnp.einsum('bsd,df->bsf', h2, w_gate[layer])
        up = jnp.einsum('bsd,df->bsf', h2, w_up[layer])
        x = x + jnp.einsum('bsf,fd->bsd', jax.nn.silu(gate) * up, w_down[layer])
    return rmsnorm(x, final_g)
```

```python
import functools
import math

import jax
import jax.numpy as jnp
from jax import lax
from jax.experimental import pallas as pl
from jax.experimental.pallas import tpu as pltpu

D_MODEL = 1024
ATTN_WIDTH = 512
N_HEADS = 8
HEAD_DIM = 64
POOL_WIDTH = 512
POOL_WINDOWS = (2, 4, 8, 16)
POOL_GROUP = 128
D_FF = 2816
EPS = 1e-6

LANES = 128
HEAD_SLOT = 128
N_BIAS_TERMS = 3
VT_ROWS = HEAD_DIM + 16
POOL_HALO = 16
LOG2E = math.log2(math.e)
NEG_BIG = -1e30

TOKEN_TILE = 512
ATTN_BLOCK = 512
FF_CHUNK = 1408
VMEM_LIMIT = 56 * 1024 * 1024

_NT = (((1,), (1,)), ((), ()))
_TN = (((0,), (0,)), ((), ()))


def _rms(x, g):
    ms = jnp.mean(x * x, axis=-1, keepdims=True)
    return x * lax.rsqrt(ms + EPS) * g


def _split_bf16(x):
    hi = x.astype(jnp.bfloat16)
    r1 = x - hi.astype(jnp.float32)
    mid = r1.astype(jnp.bfloat16)
    lo = (r1 - mid.astype(jnp.float32)).astype(jnp.bfloat16)
    return hi, mid, lo


def _in_proj_kernel(x_ref, g_ref, wqk_ref, wvt_ref, wu_ref, wf_ref, bf_ref, tri_ref, mod3_ref,
                    qp_ref, kp_ref, vt_ref, u_ref, carry_ref):
    tm = x_ref.shape[1]

    @pl.when(pl.program_id(1) == 0)
    def _():
        carry_ref[...] = jnp.zeros_like(carry_ref)

    h = _rms(x_ref[0], g_ref[...]).astype(jnp.bfloat16)
    qk = jnp.dot(h, wqk_ref[...], preferred_element_type=jnp.float32)
    vt = lax.dot_general(wvt_ref[...], h, _NT, preferred_element_type=jnp.float32)
    u_ref[0] = jnp.dot(h, wu_ref[...], preferred_element_type=jnp.float32)
    f = jnp.dot(h, wf_ref[...], preferred_element_type=jnp.float32) + bf_ref[...]

    lf = jnp.minimum(f, 0.0) - jnp.log(1.0 + jnp.exp(-jnp.abs(f)))
    pieces = jnp.concatenate(_split_bf16(lf), axis=1)
    cs3 = jnp.dot(tri_ref[...], pieces, preferred_element_type=jnp.float32)
    c = cs3[:, :LANES] + cs3[:, LANES:2 * LANES] + cs3[:, 2 * LANES:] + carry_ref[...]
    carry_ref[...] = c[tm - 1:tm, :]

    hi, mid, lo = _split_bf16(c * (-LOG2E))
    m3 = mod3_ref[...]
    e3 = jnp.where(m3 == 0, hi.astype(jnp.float32),
                   jnp.where(m3 == 1, mid.astype(jnp.float32), lo.astype(jnp.float32)))

    lane = lax.broadcasted_iota(jnp.int32, (tm, LANES), 1)
    low_half = lane < HEAD_DIM
    ones_hi = jnp.where((lane >= HEAD_DIM) & (lane < HEAD_DIM + N_BIAS_TERMS), 1.0, 0.0)
    ones_lo = jnp.where(lane < N_BIAS_TERMS, 1.0, 0.0)
    q_scale = HEAD_DIM ** -0.5 * LOG2E
    for hd in range(N_HEADS):
        pair = hd // 2
        qpair = qk[:, pair * LANES:(pair + 1) * LANES] * q_scale
        kpair = qk[:, ATTN_WIDTH + pair * LANES:ATTN_WIDTH + (pair + 1) * LANES]
        if hd % 2 == 0:
            bias = pltpu.roll(e3, HEAD_DIM - N_BIAS_TERMS * hd, 1)
            qh = jnp.where(low_half, qpair, ones_hi)
            kh = jnp.where(low_half, kpair, bias)
        else:
            bias = pltpu.roll(e3, LANES - N_BIAS_TERMS * hd, 1)
            qh = jnp.where(low_half, ones_lo, qpair)
            kh = jnp.where(low_half, bias, kpair)
        qp_ref[0, :, hd * HEAD_SLOT:(hd + 1) * HEAD_SLOT] = qh.astype(jnp.bfloat16)
        kp_ref[0, :, hd * HEAD_SLOT:(hd + 1) * HEAD_SLOT] = kh.astype(jnp.bfloat16)
        vt_ref[0, hd * VT_ROWS:hd * VT_ROWS + HEAD_DIM, :] = (
            vt[hd * HEAD_DIM:(hd + 1) * HEAD_DIM, :].astype(jnp.bfloat16))
        vt_ref[0, hd * VT_ROWS + HEAD_DIM:(hd + 1) * VT_ROWS, :] = jnp.ones(
            (VT_ROWS - HEAD_DIM, tm), jnp.bfloat16)


def _in_proj(x, g, wqk, wvt, wu, wf, bf, tri, mod3):
    B, S, D = x.shape
    tm = TOKEN_TILE
    const = lambda shape: pl.BlockSpec(shape, lambda b, j: (0,) * len(shape))
    return pl.pallas_call(
        _in_proj_kernel,
        grid=(B, S // tm),
        in_specs=[
            pl.BlockSpec((1, tm, D), lambda b, j: (b, j, 0)),
            const(g.shape), const(wqk.shape), const(wvt.shape), const(wu.shape),
            const(wf.shape), const(bf.shape), const(tri.shape), const(mod3.shape),
        ],
        out_specs=[
            pl.BlockSpec((1, tm, N_HEADS * HEAD_SLOT), lambda b, j: (b, j, 0)),
            pl.BlockSpec((1, tm, N_HEADS * HEAD_SLOT), lambda b, j: (b, j, 0)),
            pl.BlockSpec((1, N_HEADS * VT_ROWS, tm), lambda b, j: (b, 0, j)),
            pl.BlockSpec((1, tm, POOL_WIDTH), lambda b, j: (b, j, 0)),
        ],
        out_shape=[
            jax.ShapeDtypeStruct((B, S, N_HEADS * HEAD_SLOT), jnp.bfloat16),
            jax.ShapeDtypeStruct((B, S, N_HEADS * HEAD_SLOT), jnp.bfloat16),
            jax.ShapeDtypeStruct((B, N_HEADS * VT_ROWS, S), jnp.bfloat16),
            jax.ShapeDtypeStruct((B, S, POOL_WIDTH), jnp.float32),
        ],
        scratch_shapes=[pltpu.VMEM((1, LANES), jnp.float32)],
        compiler_params=pltpu.CompilerParams(
            dimension_semantics=("arbitrary", "arbitrary"), vmem_limit_bytes=VMEM_LIMIT),
        name="in_proj",
    )(x, g, wqk, wvt, wu, wf, bf, tri, mod3)


def _fox_attn_kernel(q_ref, k_ref, vt_ref, o_ref):
    S = q_ref.shape[1]
    blk = ATTN_BLOCK
    n_blocks = S // blk

    def scores(qb, i):
        kb = k_ref[0, pl.ds(pl.multiple_of(i * blk, blk), blk), :]
        return lax.dot_general(kb, qb, _NT, preferred_element_type=jnp.float32)

    def update(st, i, m, acc):
        m_new = jnp.maximum(m, jnp.max(st, axis=0, keepdims=True))
        p = jnp.exp2(st - m_new).astype(jnp.bfloat16)
        alpha = jnp.exp2(m - m_new)
        vt = vt_ref[0, :, pl.ds(pl.multiple_of(i * blk, blk), blk)]
        return m_new, alpha * acc + jnp.dot(vt, p, preferred_element_type=jnp.float32)

    def q_block(j, _):
        q0 = pl.multiple_of(j * blk, blk)
        qb = q_ref[0, pl.ds(q0, blk), :]

        def k_block(i, carry):
            return update(scores(qb, i), i, *carry)

        m0 = jnp.full((1, blk), NEG_BIG, jnp.float32)
        acc0 = jnp.zeros((VT_ROWS, blk), jnp.float32)
        m, acc = lax.fori_loop(0, j, k_block, (m0, acc0))

        key = lax.broadcasted_iota(jnp.int32, (blk, blk), 0)
        qry = lax.broadcasted_iota(jnp.int32, (blk, blk), 1)
        st = jnp.where(key <= qry, scores(qb, j), NEG_BIG)
        m, acc = update(st, j, m, acc)

        out = acc[:HEAD_DIM, :] / acc[HEAD_DIM:HEAD_DIM + 1, :]
        o_ref[0, :, pl.ds(q0, blk)] = out.astype(o_ref.dtype)
        return 0

    lax.fori_loop(0, n_blocks, q_block, 0)


def _fox_attn(qp, kp, vtp):
    B, S, _ = qp.shape
    return pl.pallas_call(
        _fox_attn_kernel,
        grid=(B, N_HEADS),
        in_specs=[
            pl.BlockSpec((1, S, HEAD_SLOT), lambda b, h: (b, 0, h)),
            pl.BlockSpec((1, S, HEAD_SLOT), lambda b, h: (b, 0, h)),
            pl.BlockSpec((1, VT_ROWS, S), lambda b, h: (b, h, 0)),
        ],
        out_specs=pl.BlockSpec((1, HEAD_DIM, S), lambda b, h: (b, h, 0)),
        out_shape=jax.ShapeDtypeStruct((B, ATTN_WIDTH, S), jnp.bfloat16),
        compiler_params=pltpu.CompilerParams(
            dimension_semantics=("arbitrary", "arbitrary"), vmem_limit_bytes=VMEM_LIMIT),
        name="fox_attn",
    )(qp, kp, vtp)


def _mix_ffn_kernel(x_ref, at_ref, u_ref, uprev_ref, wpool_ref, pscale_ref, wout_ref, g2_ref,
                    wg_ref, wu_ref, wd_ref, gf_ref, y_ref, ext_ref, *, final_norm):
    tm = x_ref.shape[1]
    j = pl.program_id(1)

    u = u_ref[0]
    ext_ref[POOL_HALO:, :] = u
    ext_ref[:POOL_HALO, :] = jnp.where(j > 0, uprev_ref[0], 0.0)

    pos = lax.broadcasted_iota(jnp.int32, (tm, POOL_GROUP), 0) + j * tm
    pooled = []
    for gi, w in enumerate(POOL_WINDOWS):
        cols = slice(gi * POOL_GROUP, (gi + 1) * POOL_GROUP)
        wsum = ext_ref[POOL_HALO:, cols]
        for d in range(1, w):
            wsum = wsum + ext_ref[POOL_HALO - d:POOL_HALO - d + tm, cols]
        count = jnp.minimum(pos + 1, w).astype(jnp.float32)
        pg = (wsum / count - u[:, cols]).astype(jnp.bfloat16)
        mg = jnp.dot(pg, wpool_ref[gi], preferred_element_type=jnp.float32)
        pooled.append((mg * pscale_ref[:, cols]).astype(jnp.bfloat16))
    pool = jnp.concatenate(pooled, axis=1)

    x1 = (x_ref[0]
          + lax.dot_general(at_ref[0], wout_ref[:ATTN_WIDTH, :], _TN,
                            preferred_element_type=jnp.float32)
          + jnp.dot(pool, wout_ref[ATTN_WIDTH:, :], preferred_element_type=jnp.float32))

    h2 = _rms(x1, g2_ref[...]).astype(jnp.bfloat16)
    ffn = jnp.zeros((tm, D_MODEL), jnp.float32)
    for c0 in range(0, D_FF, FF_CHUNK):
        gate = jnp.dot(h2, wg_ref[:, c0:c0 + FF_CHUNK], preferred_element_type=jnp.float32)
        up = jnp.dot(h2, wu_ref[:, c0:c0 + FF_CHUNK], preferred_element_type=jnp.float32)
        act = (gate / (1.0 + jnp.exp(-gate)) * up).astype(jnp.bfloat16)
        ffn = ffn + jnp.dot(act, wd_ref[c0:c0 + FF_CHUNK, :], preferred_element_type=jnp.float32)

    x2 = x1 + ffn
    y_ref[0] = _rms(x2, gf_ref[...]) if final_norm else x2


def _mix_ffn(x, attn_t, u, wpool, pscale, wout, g2, wg, wu, wd, gf, final_norm):
    B, S, D = x.shape
    tm = TOKEN_TILE
    halo_blocks = tm // POOL_HALO
    const = lambda shape: pl.BlockSpec(shape, lambda b, j: (0,) * len(shape),
                                       pipeline_mode=pl.Buffered(1))
    return pl.pallas_call(
        functools.partial(_mix_ffn_kernel, final_norm=final_norm),
        grid=(B, S // tm),
        in_specs=[
            pl.BlockSpec((1, tm, D), lambda b, j: (b, j, 0)),
            pl.BlockSpec((1, ATTN_WIDTH, tm), lambda b, j: (b, 0, j)),
            pl.BlockSpec((1, tm, POOL_WIDTH), lambda b, j: (b, j, 0)),
            pl.BlockSpec((1, POOL_HALO, POOL_WIDTH),
                         lambda b, j: (b, jnp.maximum(j * halo_blocks - 1, 0), 0)),
            const(wpool.shape), const(pscale.shape), const(wout.shape), const(g2.shape),
            const(wg.shape), const(wu.shape), const(wd.shape), const(gf.shape),
        ],
        out_specs=pl.BlockSpec((1, tm, D), lambda b, j: (b, j, 0)),
        out_shape=jax.ShapeDtypeStruct((B, S, D), jnp.float32),
        scratch_shapes=[pltpu.VMEM((tm + POOL_HALO, POOL_WIDTH), jnp.float32)],
        compiler_params=pltpu.CompilerParams(
            dimension_semantics=("arbitrary", "arbitrary"), vmem_limit_bytes=VMEM_LIMIT),
        name="mix_ffn",
    )(x, attn_t, u, u, wpool, pscale, wout, g2, wg, wu, wd, gf)


def kernel(x, norm1_g, w_in, b_forget, w_pool, pool_scale, w_out, norm2_g, w_gate, w_up, w_down, final_g):
    depth = w_in.shape[0]
    bf16 = jnp.bfloat16
    a0 = ATTN_WIDTH
    tri = jnp.tril(jnp.ones((TOKEN_TILE, TOKEN_TILE), bf16))
    lane = jnp.arange(LANES)
    bias_lane = lane < N_BIAS_TERMS * N_HEADS
    mod3 = jnp.where(bias_lane, lane % N_BIAS_TERMS, N_BIAS_TERMS).astype(jnp.int32)[None, :]
    head_of_lane = jnp.minimum(lane // N_BIAS_TERMS, N_HEADS - 1)
    for layer in range(depth):
        w = w_in[layer]
        wqk = w[:, :2 * a0].astype(bf16)
        wvt = w[:, 2 * a0:3 * a0].T.astype(bf16)
        wf = jnp.where(bias_lane[None, :], w[:, 3 * a0:3 * a0 + N_HEADS][:, head_of_lane], 0.0).astype(bf16)
        bf = jnp.where(bias_lane, b_forget[layer][head_of_lane], 0.0).astype(jnp.float32)[None, :]
        wu_in = w[:, 3 * a0 + N_HEADS:].astype(bf16)
        qp, kp, vtp, u = _in_proj(x, norm1_g[layer][None, :], wqk, wvt, wu_in, wf, bf, tri, mod3)
        attn_t = _fox_attn(qp, kp, vtp)
        x = _mix_ffn(x, attn_t, u, w_pool[layer].astype(bf16), pool_scale[layer][None, :],
                     w_out[layer].astype(bf16), norm2_g[layer][None, :], w_gate[layer].astype(bf16),
                     w_up[layer].astype(bf16), w_down[layer].astype(bf16), final_g[None, :],
                     final_norm=layer == depth - 1)
    return x
```

```python
import functools
import math

import jax
import jax.numpy as jnp
from jax import lax
from jax.experimental import pallas as pl
from jax.experimental.pallas import tpu as pltpu

D_MODEL = 1024
ATTN_WIDTH = 512
N_HEADS = 8
HEAD_DIM = 64
POOL_WIDTH = 512
POOL_WINDOWS = (2, 4, 8, 16)
POOL_GROUP = 128
D_FF = 2816
EPS = 1e-6

LANES = 128
HEAD_SLOT = 128
N_BIAS_TERMS = 3
VT_ROWS = HEAD_DIM + 16
POOL_HALO = 16
LOG2E = math.log2(math.e)
NEG_BIG = -1e30

TOKEN_TILE = 512
ATTN_BLOCK = 512
ATTN_UNIT = 256
ATTN_HEADS_PER_STEP = 2
ATTN_LOOKAHEAD = 5
FF_CHUNK = 1408
VMEM_LIMIT = 56 * 1024 * 1024

_NT = (((1,), (1,)), ((), ()))
_TN = (((0,), (0,)), ((), ()))


def _rms(x, g):
    ms = jnp.mean(x * x, axis=-1, keepdims=True)
    return x * lax.rsqrt(ms + EPS) * g


def _split_bf16(x):
    hi = x.astype(jnp.bfloat16)
    r1 = x - hi.astype(jnp.float32)
    mid = r1.astype(jnp.bfloat16)
    lo = (r1 - mid.astype(jnp.float32)).astype(jnp.bfloat16)
    return hi, mid, lo


def _in_proj_kernel(x_ref, g_ref, wqk_ref, wvt_ref, wu_ref, wf_ref, bf_ref, tri_ref, mod3_ref,
                    qp_ref, kp_ref, vt_ref, u_ref, carry_ref):
    tm = x_ref.shape[1]

    @pl.when(pl.program_id(1) == 0)
    def _():
        carry_ref[...] = jnp.zeros_like(carry_ref)

    h = _rms(x_ref[0], g_ref[...]).astype(jnp.bfloat16)
    qk = jnp.dot(h, wqk_ref[...], preferred_element_type=jnp.float32)
    vt = lax.dot_general(wvt_ref[...], h, _NT, preferred_element_type=jnp.float32)
    u_ref[0] = jnp.dot(h, wu_ref[...], preferred_element_type=jnp.float32)
    f = jnp.dot(h, wf_ref[...], preferred_element_type=jnp.float32) + bf_ref[...]

    lf = jnp.minimum(f, 0.0) - jnp.log(1.0 + jnp.exp(-jnp.abs(f)))
    pieces = jnp.concatenate(_split_bf16(lf), axis=1)
    cs3 = jnp.dot(tri_ref[...], pieces, preferred_element_type=jnp.float32)
    c = cs3[:, :LANES] + cs3[:, LANES:2 * LANES] + cs3[:, 2 * LANES:] + carry_ref[...]
    carry_ref[...] = c[tm - 1:tm, :]

    hi, mid, lo = _split_bf16(c * (-LOG2E))
    m3 = mod3_ref[...]
    e3 = jnp.where(m3 == 0, hi.astype(jnp.float32),
                   jnp.where(m3 == 1, mid.astype(jnp.float32), lo.astype(jnp.float32)))

    lane = lax.broadcasted_iota(jnp.int32, (tm, LANES), 1)
    low_half = lane < HEAD_DIM
    ones_hi = jnp.where((lane >= HEAD_DIM) & (lane < HEAD_DIM + N_BIAS_TERMS), 1.0, 0.0)
    ones_lo = jnp.where(lane < N_BIAS_TERMS, 1.0, 0.0)
    q_scale = HEAD_DIM ** -0.5 * LOG2E
    for hd in range(N_HEADS):
        pair = hd // 2
        qpair = qk[:, pair * LANES:(pair + 1) * LANES] * q_scale
        kpair = qk[:, ATTN_WIDTH + pair * LANES:ATTN_WIDTH + (pair + 1) * LANES]
        if hd % 2 == 0:
            bias = pltpu.roll(e3, HEAD_DIM - N_BIAS_TERMS * hd, 1)
            qh = jnp.where(low_half, qpair, ones_hi)
            kh = jnp.where(low_half, kpair, bias)
        else:
            bias = pltpu.roll(e3, LANES - N_BIAS_TERMS * hd, 1)
            qh = jnp.where(low_half, ones_lo, qpair)
            kh = jnp.where(low_half, bias, kpair)
        qp_ref[0, :, hd * HEAD_SLOT:(hd + 1) * HEAD_SLOT] = qh.astype(jnp.bfloat16)
        kp_ref[0, :, hd * HEAD_SLOT:(hd + 1) * HEAD_SLOT] = kh.astype(jnp.bfloat16)
        vt_ref[0, hd * VT_ROWS:hd * VT_ROWS + HEAD_DIM, :] = (
            vt[hd * HEAD_DIM:(hd + 1) * HEAD_DIM, :].astype(jnp.bfloat16))
        vt_ref[0, hd * VT_ROWS + HEAD_DIM:(hd + 1) * VT_ROWS, :] = jnp.ones(
            (VT_ROWS - HEAD_DIM, tm), jnp.bfloat16)


def _in_proj(x, g, wqk, wvt, wu, wf, bf, tri, mod3):
    B, S, D = x.shape
    tm = TOKEN_TILE
    const = lambda shape: pl.BlockSpec(shape, lambda b, j: (0,) * len(shape))
    return pl.pallas_call(
        _in_proj_kernel,
        grid=(B, S // tm),
        in_specs=[
            pl.BlockSpec((1, tm, D), lambda b, j: (b, j, 0)),
            const(g.shape), const(wqk.shape), const(wvt.shape), const(wu.shape),
            const(wf.shape), const(bf.shape), const(tri.shape), const(mod3.shape),
        ],
        out_specs=[
            pl.BlockSpec((1, tm, N_HEADS * HEAD_SLOT), lambda b, j: (b, j, 0)),
            pl.BlockSpec((1, tm, N_HEADS * HEAD_SLOT), lambda b, j: (b, j, 0)),
            pl.BlockSpec((1, N_HEADS * VT_ROWS, tm), lambda b, j: (b, 0, j)),
            pl.BlockSpec((1, tm, POOL_WIDTH), lambda b, j: (b, j, 0)),
        ],
        out_shape=[
            jax.ShapeDtypeStruct((B, S, N_HEADS * HEAD_SLOT), jnp.bfloat16),
            jax.ShapeDtypeStruct((B, S, N_HEADS * HEAD_SLOT), jnp.bfloat16),
            jax.ShapeDtypeStruct((B, N_HEADS * VT_ROWS, S), jnp.bfloat16),
            jax.ShapeDtypeStruct((B, S, POOL_WIDTH), jnp.float32),
        ],
        scratch_shapes=[pltpu.VMEM((1, LANES), jnp.float32)],
        compiler_params=pltpu.CompilerParams(
            dimension_semantics=("arbitrary", "arbitrary"), vmem_limit_bytes=VMEM_LIMIT),
        name="in_proj",
    )(x, g, wqk, wvt, wu, wf, bf, tri, mod3)


def _fox_attn_kernel(q_ref, k_ref, vt_ref, o_ref, m_ref, acc_ref):
    S = q_ref.shape[1]
    blk, sub = ATTN_BLOCK, ATTN_UNIT
    heads = q_ref.shape[2] // HEAD_SLOT
    groups = blk // sub

    def scores(hd, qg, q0, k0, masked):
        lanes = slice(hd * HEAD_SLOT, (hd + 1) * HEAD_SLOT)
        qt = q_ref[0, pl.ds(q0 + qg * sub, sub), lanes]
        kt = k_ref[0, pl.ds(k0, sub), lanes]
        st = lax.dot_general(kt, qt, _NT, preferred_element_type=jnp.float32)
        if masked:
            key = lax.broadcasted_iota(jnp.int32, (sub, sub), 0)
            qry = lax.broadcasted_iota(jnp.int32, (sub, sub), 1)
            st = jnp.where(key <= qry, st, NEG_BIG)
        return st

    def update(hd, k0, st, m_old, acc_old):
        m_new = jnp.maximum(m_old, jnp.max(st, axis=0, keepdims=True))
        p = jnp.exp2(st - m_new).astype(jnp.bfloat16)
        alpha = jnp.exp2(m_old - m_new)
        vt = vt_ref[0, hd * VT_ROWS:(hd + 1) * VT_ROWS, pl.ds(k0, sub)]
        return m_new, alpha * acc_old + jnp.dot(vt, p, preferred_element_type=jnp.float32)

    def run_tiles(state, q0, work):
        pending = []
        for n in range(len(work) + ATTN_LOOKAHEAD):
            if n < len(work):
                hd, qg, k0, masked = work[n]
                pending.append(scores(hd, qg, q0, k0, masked))
            if n >= ATTN_LOOKAHEAD:
                hd, qg, k0, _ = work[n - ATTN_LOOKAHEAD]
                state[hd, qg] = update(hd, k0, pending.pop(0), *state[hd, qg])

    tiles = [(hd, qg) for hd in range(heads) for qg in range(groups)]

    def load_state():
        return {(hd, qg): (m_ref[hd, :, qg * sub:(qg + 1) * sub], acc_ref[hd, :, qg * sub:(qg + 1) * sub])
                for hd, qg in tiles}

    def store_state(state):
        for (hd, qg), (m, acc) in state.items():
            m_ref[hd, :, qg * sub:(qg + 1) * sub] = m
            acc_ref[hd, :, qg * sub:(qg + 1) * sub] = acc

    def q_block(j, _):
        q0 = pl.multiple_of(j * blk, blk)
        m_ref[...] = jnp.full(m_ref.shape, NEG_BIG, jnp.float32)
        acc_ref[...] = jnp.zeros(acc_ref.shape, jnp.float32)

        def k_block(i, _):
            k0 = pl.multiple_of(i * blk, blk)
            state = load_state()
            run_tiles(state, q0, [(hd, qg, k0 + ks * sub, False)
                                  for ks in range(groups) for hd, qg in tiles])
            store_state(state)
            return 0

        lax.fori_loop(0, j, k_block, 0)

        state = load_state()
        run_tiles(state, q0, [(hd, qg, q0 + ks * sub, qg == ks)
                              for ks in range(groups) for hd, qg in tiles if qg >= ks])
        store_state(state)

        for hd in range(heads):
            acc = acc_ref[hd]
            out = acc[:HEAD_DIM, :] / acc[HEAD_DIM:HEAD_DIM + 1, :]
            o_ref[0, hd * HEAD_DIM:(hd + 1) * HEAD_DIM, pl.ds(q0, blk)] = out.astype(o_ref.dtype)
        return 0

    lax.fori_loop(0, S // blk, q_block, 0)


def _fox_attn(qp, kp, vtp):
    B, S, _ = qp.shape
    hp = ATTN_HEADS_PER_STEP
    return pl.pallas_call(
        _fox_attn_kernel,
        grid=(B, N_HEADS // hp),
        in_specs=[
            pl.BlockSpec((1, S, hp * HEAD_SLOT), lambda b, h: (b, 0, h)),
            pl.BlockSpec((1, S, hp * HEAD_SLOT), lambda b, h: (b, 0, h)),
            pl.BlockSpec((1, hp * VT_ROWS, S), lambda b, h: (b, h, 0)),
        ],
        out_specs=pl.BlockSpec((1, hp * HEAD_DIM, S), lambda b, h: (b, h, 0)),
        out_shape=jax.ShapeDtypeStruct((B, ATTN_WIDTH, S), jnp.bfloat16),
        scratch_shapes=[pltpu.VMEM((hp, 1, ATTN_BLOCK), jnp.float32),
                        pltpu.VMEM((hp, VT_ROWS, ATTN_BLOCK), jnp.float32)],
        compiler_params=pltpu.CompilerParams(
            dimension_semantics=("arbitrary", "arbitrary"), vmem_limit_bytes=VMEM_LIMIT),
        name="fox_attn",
    )(qp, kp, vtp)


def _mix_ffn_kernel(x_ref, at_ref, u_ref, uprev_ref, wpool_ref, pscale_ref, wout_ref, g2_ref,
                    wg_ref, wu_ref, wd_ref, gf_ref, y_ref, ext_ref, *, final_norm):
    tm = x_ref.shape[1]
    j = pl.program_id(1)

    u = u_ref[0]
    ext_ref[POOL_HALO:, :] = u
    ext_ref[:POOL_HALO, :] = jnp.where(j > 0, uprev_ref[0], 0.0)

    pos = lax.broadcasted_iota(jnp.int32, (tm, POOL_GROUP), 0) + j * tm
    pooled = []
    for gi, w in enumerate(POOL_WINDOWS):
        cols = slice(gi * POOL_GROUP, (gi + 1) * POOL_GROUP)
        wsum = ext_ref[POOL_HALO:, cols]
        for d in range(1, w):
            wsum = wsum + ext_ref[POOL_HALO - d:POOL_HALO - d + tm, cols]
        count = jnp.minimum(pos + 1, w).astype(jnp.float32)
        pg = (wsum / count - u[:, cols]).astype(jnp.bfloat16)
        mg = jnp.dot(pg, wpool_ref[gi], preferred_element_type=jnp.float32)
        pooled.append((mg * pscale_ref[:, cols]).astype(jnp.bfloat16))
    pool = jnp.concatenate(pooled, axis=1)

    x1 = (x_ref[0]
          + lax.dot_general(at_ref[0], wout_ref[:ATTN_WIDTH, :], _TN,
                            preferred_element_type=jnp.float32)
          + jnp.dot(pool, wout_ref[ATTN_WIDTH:, :], preferred_element_type=jnp.float32))

    h2 = _rms(x1, g2_ref[...]).astype(jnp.bfloat16)
    ffn = jnp.zeros((tm, D_MODEL), jnp.float32)
    for c0 in range(0, D_FF, FF_CHUNK):
        gate = jnp.dot(h2, wg_ref[:, c0:c0 + FF_CHUNK], preferred_element_type=jnp.float32)
        up = jnp.dot(h2, wu_ref[:, c0:c0 + FF_CHUNK], preferred_element_type=jnp.float32)
        act = (gate / (1.0 + jnp.exp(-gate)) * up).astype(jnp.bfloat16)
        ffn = ffn + jnp.dot(act, wd_ref[c0:c0 + FF_CHUNK, :], preferred_element_type=jnp.float32)

    x2 = x1 + ffn
    y_ref[0] = _rms(x2, gf_ref[...]) if final_norm else x2


def _mix_ffn(x, attn_t, u, wpool, pscale, wout, g2, wg, wu, wd, gf, final_norm):
    B, S, D = x.shape
    tm = TOKEN_TILE
    halo_blocks = tm // POOL_HALO
    const = lambda shape: pl.BlockSpec(shape, lambda b, j: (0,) * len(shape),
                                       pipeline_mode=pl.Buffered(1))
    return pl.pallas_call(
        functools.partial(_mix_ffn_kernel, final_norm=final_norm),
        grid=(B, S // tm),
        in_specs=[
            pl.BlockSpec((1, tm, D), lambda b, j: (b, j, 0)),
            pl.BlockSpec((1, ATTN_WIDTH, tm), lambda b, j: (b, 0, j)),
            pl.BlockSpec((1, tm, POOL_WIDTH), lambda b, j: (b, j, 0)),
            pl.BlockSpec((1, POOL_HALO, POOL_WIDTH),
                         lambda b, j: (b, jnp.maximum(j * halo_blocks - 1, 0), 0)),
            const(wpool.shape), const(pscale.shape), const(wout.shape), const(g2.shape),
            const(wg.shape), const(wu.shape), const(wd.shape), const(gf.shape),
        ],
        out_specs=pl.BlockSpec((1, tm, D), lambda b, j: (b, j, 0)),
        out_shape=jax.ShapeDtypeStruct((B, S, D), jnp.float32),
        scratch_shapes=[pltpu.VMEM((tm + POOL_HALO, POOL_WIDTH), jnp.float32)],
        compiler_params=pltpu.CompilerParams(
            dimension_semantics=("arbitrary", "arbitrary"), vmem_limit_bytes=VMEM_LIMIT),
        name="mix_ffn",
    )(x, attn_t, u, u, wpool, pscale, wout, g2, wg, wu, wd, gf)


def kernel(x, norm1_g, w_in, b_forget, w_pool, pool_scale, w_out, norm2_g, w_gate, w_up, w_down, final_g):
    depth = w_in.shape[0]
    bf16 = jnp.bfloat16
    a0 = ATTN_WIDTH
    tri = jnp.tril(jnp.ones((TOKEN_TILE, TOKEN_TILE), bf16))
    lane = jnp.arange(LANES)
    bias_lane = lane < N_BIAS_TERMS * N_HEADS
    mod3 = jnp.where(bias_lane, lane % N_BIAS_TERMS, N_BIAS_TERMS).astype(jnp.int32)[None, :]
    head_of_lane = jnp.minimum(lane // N_BIAS_TERMS, N_HEADS - 1)
    for layer in range(depth):
        w = w_in[layer]
        wqk = w[:, :2 * a0].astype(bf16)
        wvt = w[:, 2 * a0:3 * a0].T.astype(bf16)
        wf = jnp.where(bias_lane[None, :], w[:, 3 * a0:3 * a0 + N_HEADS][:, head_of_lane], 0.0).astype(bf16)
        bf = jnp.where(bias_lane, b_forget[layer][head_of_lane], 0.0).astype(jnp.float32)[None, :]
        wu_in = w[:, 3 * a0 + N_HEADS:].astype(bf16)
        qp, kp, vtp, u = _in_proj(x, norm1_g[layer][None, :], wqk, wvt, wu_in, wf, bf, tri, mod3)
        attn_t = _fox_attn(qp, kp, vtp)
        x = _mix_ffn(x, attn_t, u, w_pool[layer].astype(bf16), pool_scale[layer][None, :],
                     w_out[layer].astype(bf16), norm2_g[layer][None, :], w_gate[layer].astype(bf16),
                     w_up[layer].astype(bf16), w_down[layer].astype(bf16), final_g[None, :],
                     final_norm=layer == depth - 1)
    return x
```

```python
import functools
import math

import jax
import jax.numpy as jnp
from jax import lax
from jax.experimental import pallas as pl
from jax.experimental.pallas import tpu as pltpu

D_MODEL = 1024
ATTN_WIDTH = 512
N_HEADS = 8
HEAD_DIM = 64
POOL_WIDTH = 512
POOL_WINDOWS = (2, 4, 8, 16)
POOL_GROUP = 128
D_FF = 2816
EPS = 1e-6

LANES = 128
HEAD_SLOT = 128
N_BIAS_TERMS = 3
VT_ROWS = HEAD_DIM + 16
POOL_HALO = 16
LOG2E = math.log2(math.e)
NEG_BIG = -1e30

TOKEN_TILE = 512
ATTN_BLOCK = 512
ATTN_UNIT = 256
ATTN_HEADS_PER_STEP = 2
ATTN_ISSUE_AHEAD = 2
FF_CHUNK = 1408
VMEM_LIMIT = 56 * 1024 * 1024

_NT = (((1,), (1,)), ((), ()))
_TN = (((0,), (0,)), ((), ()))


def _rms(x, g):
    ms = jnp.mean(x * x, axis=-1, keepdims=True)
    return x * lax.rsqrt(ms + EPS) * g


def _split_bf16(x):
    hi = x.astype(jnp.bfloat16)
    r1 = x - hi.astype(jnp.float32)
    mid = r1.astype(jnp.bfloat16)
    lo = (r1 - mid.astype(jnp.float32)).astype(jnp.bfloat16)
    return hi, mid, lo


def _in_proj_kernel(x_ref, g_ref, wqk_ref, wvt_ref, wu_ref, wf_ref, bf_ref, tri_ref, mod3_ref,
                    qp_ref, kp_ref, vt_ref, u_ref, carry_ref):
    tm = x_ref.shape[1]

    @pl.when(pl.program_id(1) == 0)
    def _():
        carry_ref[...] = jnp.zeros_like(carry_ref)

    h = _rms(x_ref[0], g_ref[...]).astype(jnp.bfloat16)
    qk = jnp.dot(h, wqk_ref[...], preferred_element_type=jnp.float32)
    vt = lax.dot_general(wvt_ref[...], h, _NT, preferred_element_type=jnp.float32)
    u_ref[0] = jnp.dot(h, wu_ref[...], preferred_element_type=jnp.float32)
    f = jnp.dot(h, wf_ref[...], preferred_element_type=jnp.float32) + bf_ref[...]

    lf = jnp.minimum(f, 0.0) - jnp.log(1.0 + jnp.exp(-jnp.abs(f)))
    pieces = jnp.concatenate(_split_bf16(lf), axis=1)
    cs3 = jnp.dot(tri_ref[...], pieces, preferred_element_type=jnp.float32)
    c = cs3[:, :LANES] + cs3[:, LANES:2 * LANES] + cs3[:, 2 * LANES:] + carry_ref[...]
    carry_ref[...] = c[tm - 1:tm, :]

    hi, mid, lo = _split_bf16(c * (-LOG2E))
    m3 = mod3_ref[...]
    e3 = jnp.where(m3 == 0, hi.astype(jnp.float32),
                   jnp.where(m3 == 1, mid.astype(jnp.float32), lo.astype(jnp.float32)))

    lane = lax.broadcasted_iota(jnp.int32, (tm, LANES), 1)
    low_half = lane < HEAD_DIM
    ones_hi = jnp.where((lane >= HEAD_DIM) & (lane < HEAD_DIM + N_BIAS_TERMS), 1.0, 0.0)
    ones_lo = jnp.where(lane < N_BIAS_TERMS, 1.0, 0.0)
    q_scale = HEAD_DIM ** -0.5 * LOG2E
    for hd in range(N_HEADS):
        pair = hd // 2
        qpair = qk[:, pair * LANES:(pair + 1) * LANES] * q_scale
        kpair = qk[:, ATTN_WIDTH + pair * LANES:ATTN_WIDTH + (pair + 1) * LANES]
        if hd % 2 == 0:
            bias = pltpu.roll(e3, HEAD_DIM - N_BIAS_TERMS * hd, 1)
            qh = jnp.where(low_half, qpair, ones_hi)
            kh = jnp.where(low_half, kpair, bias)
        else:
            bias = pltpu.roll(e3, LANES - N_BIAS_TERMS * hd, 1)
            qh = jnp.where(low_half, ones_lo, qpair)
            kh = jnp.where(low_half, bias, kpair)
        qp_ref[0, :, hd * HEAD_SLOT:(hd + 1) * HEAD_SLOT] = qh.astype(jnp.bfloat16)
        kp_ref[0, :, hd * HEAD_SLOT:(hd + 1) * HEAD_SLOT] = kh.astype(jnp.bfloat16)
        vt_ref[0, hd * VT_ROWS:hd * VT_ROWS + HEAD_DIM, :] = (
            vt[hd * HEAD_DIM:(hd + 1) * HEAD_DIM, :].astype(jnp.bfloat16))
        vt_ref[0, hd * VT_ROWS + HEAD_DIM:(hd + 1) * VT_ROWS, :] = jnp.ones(
            (VT_ROWS - HEAD_DIM, tm), jnp.bfloat16)


def _in_proj(x, g, wqk, wvt, wu, wf, bf, tri, mod3):
    B, S, D = x.shape
    tm = TOKEN_TILE
    const = lambda shape: pl.BlockSpec(shape, lambda b, j: (0,) * len(shape))
    return pl.pallas_call(
        _in_proj_kernel,
        grid=(B, S // tm),
        in_specs=[
            pl.BlockSpec((1, tm, D), lambda b, j: (b, j, 0)),
            const(g.shape), const(wqk.shape), const(wvt.shape), const(wu.shape),
            const(wf.shape), const(bf.shape), const(tri.shape), const(mod3.shape),
        ],
        out_specs=[
            pl.BlockSpec((1, tm, N_HEADS * HEAD_SLOT), lambda b, j: (b, j, 0)),
            pl.BlockSpec((1, tm, N_HEADS * HEAD_SLOT), lambda b, j: (b, j, 0)),
            pl.BlockSpec((1, N_HEADS * VT_ROWS, tm), lambda b, j: (b, 0, j)),
            pl.BlockSpec((1, tm, POOL_WIDTH), lambda b, j: (b, j, 0)),
        ],
        out_shape=[
            jax.ShapeDtypeStruct((B, S, N_HEADS * HEAD_SLOT), jnp.bfloat16),
            jax.ShapeDtypeStruct((B, S, N_HEADS * HEAD_SLOT), jnp.bfloat16),
            jax.ShapeDtypeStruct((B, N_HEADS * VT_ROWS, S), jnp.bfloat16),
            jax.ShapeDtypeStruct((B, S, POOL_WIDTH), jnp.float32),
        ],
        scratch_shapes=[pltpu.VMEM((1, LANES), jnp.float32)],
        compiler_params=pltpu.CompilerParams(
            dimension_semantics=("arbitrary", "arbitrary"), vmem_limit_bytes=VMEM_LIMIT),
        name="in_proj",
    )(x, g, wqk, wvt, wu, wf, bf, tri, mod3)


def _fox_attn_kernel(q_ref, k_ref, vt_ref, o_ref, m_ref, acc_ref, pend_ref):
    S = q_ref.shape[1]
    blk, sub = ATTN_BLOCK, ATTN_UNIT
    heads = q_ref.shape[2] // HEAD_SLOT
    groups = blk // sub
    n_q = S // blk
    tiles = [(ks, hd, qg) for ks in range(groups) for hd in range(heads) for qg in range(groups)]

    def scores(hd, qg, q0, k0):
        lanes = slice(hd * HEAD_SLOT, (hd + 1) * HEAD_SLOT)
        qt = q_ref[0, pl.ds(q0 + qg * sub, sub), lanes]
        kt = k_ref[0, pl.ds(k0, sub), lanes]
        return lax.dot_general(kt, qt, _NT, preferred_element_type=jnp.float32)

    def update(hd, k0, st, m_old, acc_old):
        m_new = jnp.maximum(m_old, jnp.max(st, axis=0, keepdims=True))
        p = jnp.exp2(st - m_new).astype(jnp.bfloat16)
        alpha = jnp.exp2(m_old - m_new)
        vt = vt_ref[0, hd * VT_ROWS:(hd + 1) * VT_ROWS, pl.ds(k0, sub)]
        return m_new, alpha * acc_old + jnp.dot(vt, p, preferred_element_type=jnp.float32)

    def step(slot, k0, nxt_q0, nxt_k0, diag):
        for static_slot in range(2):
            @pl.when(slot == static_slot)
            def _():
                static_step(static_slot, k0, nxt_q0, nxt_k0, diag)

    def static_step(slot, k0, nxt_q0, nxt_k0, diag):
        state = {(hd, qg): (m_ref[hd, :, qg * sub:(qg + 1) * sub],
                            acc_ref[hd, :, qg * sub:(qg + 1) * sub])
                 for hd in range(heads) for qg in range(groups)}
        def issue(n):
            ks, hd, qg = tiles[n]
            pend_ref[1 - slot, n] = scores(hd, qg, nxt_q0, nxt_k0 + ks * sub)

        for n in range(min(ATTN_ISSUE_AHEAD, len(tiles))):
            issue(n)
        for n, (ks, hd, qg) in enumerate(tiles):
            if n + ATTN_ISSUE_AHEAD < len(tiles):
                issue(n + ATTN_ISSUE_AHEAD)
            if diag and qg < ks:
                continue
            st = pend_ref[slot, n]
            if diag and qg == ks:
                key = lax.broadcasted_iota(jnp.int32, (sub, sub), 0)
                qry = lax.broadcasted_iota(jnp.int32, (sub, sub), 1)
                st = jnp.where(key <= qry, st, NEG_BIG)
            state[hd, qg] = update(hd, k0 + ks * sub, st, *state[hd, qg])
        for (hd, qg), (m, acc) in state.items():
            m_ref[hd, :, qg * sub:(qg + 1) * sub] = m
            acc_ref[hd, :, qg * sub:(qg + 1) * sub] = acc

    def q_block(j, slot):
        q0 = pl.multiple_of(j * blk, blk)
        m_ref[...] = jnp.full(m_ref.shape, NEG_BIG, jnp.float32)
        acc_ref[...] = jnp.zeros(acc_ref.shape, jnp.float32)

        def k_block(i, slot):
            k0 = pl.multiple_of(i * blk, blk)
            step(slot, k0, q0, k0 + blk, diag=False)
            return 1 - slot

        slot = lax.fori_loop(0, j, k_block, slot)
        nxt_q0 = pl.multiple_of(jnp.minimum(j + 1, n_q - 1) * blk, blk)
        step(slot, q0, nxt_q0, 0, diag=True)

        for hd in range(heads):
            acc = acc_ref[hd]
            out = acc[:HEAD_DIM, :] / acc[HEAD_DIM:HEAD_DIM + 1, :]
            o_ref[0, hd * HEAD_DIM:(hd + 1) * HEAD_DIM, pl.ds(q0, blk)] = out.astype(o_ref.dtype)
        return 1 - slot

    for n, (ks, hd, qg) in enumerate(tiles):
        pend_ref[0, n] = scores(hd, qg, 0, ks * sub)
    lax.fori_loop(0, n_q, q_block, 0)


def _fox_attn(qp, kp, vtp):
    B, S, _ = qp.shape
    hp = ATTN_HEADS_PER_STEP
    return pl.pallas_call(
        _fox_attn_kernel,
        grid=(B, N_HEADS // hp),
        in_specs=[
            pl.BlockSpec((1, S, hp * HEAD_SLOT), lambda b, h: (b, 0, h)),
            pl.BlockSpec((1, S, hp * HEAD_SLOT), lambda b, h: (b, 0, h)),
            pl.BlockSpec((1, hp * VT_ROWS, S), lambda b, h: (b, h, 0)),
        ],
        out_specs=pl.BlockSpec((1, hp * HEAD_DIM, S), lambda b, h: (b, h, 0)),
        out_shape=jax.ShapeDtypeStruct((B, ATTN_WIDTH, S), jnp.bfloat16),
        scratch_shapes=[pltpu.VMEM((hp, 1, ATTN_BLOCK), jnp.float32),
                        pltpu.VMEM((hp, VT_ROWS, ATTN_BLOCK), jnp.float32),
                        pltpu.VMEM((2, hp * (ATTN_BLOCK // ATTN_UNIT) ** 2, ATTN_UNIT, ATTN_UNIT),
                                   jnp.float32)],
        compiler_params=pltpu.CompilerParams(
            dimension_semantics=("arbitrary", "arbitrary"), vmem_limit_bytes=VMEM_LIMIT),
        name="fox_attn",
    )(qp, kp, vtp)


def _mix_ffn_kernel(x_ref, at_ref, u_ref, uprev_ref, wpool_ref, pscale_ref, wout_ref, g2_ref,
                    wg_ref, wu_ref, wd_ref, gf_ref, y_ref, ext_ref, *, final_norm):
    tm = x_ref.shape[1]
    j = pl.program_id(1)

    u = u_ref[0]
    ext_ref[POOL_HALO:, :] = u
    ext_ref[:POOL_HALO, :] = jnp.where(j > 0, uprev_ref[0], 0.0)

    pos = lax.broadcasted_iota(jnp.int32, (tm, POOL_GROUP), 0) + j * tm
    pooled = []
    for gi, w in enumerate(POOL_WINDOWS):
        cols = slice(gi * POOL_GROUP, (gi + 1) * POOL_GROUP)
        wsum = ext_ref[POOL_HALO:, cols]
        for d in range(1, w):
            wsum = wsum + ext_ref[POOL_HALO - d:POOL_HALO - d + tm, cols]
        count = jnp.minimum(pos + 1, w).astype(jnp.float32)
        pg = (wsum / count - u[:, cols]).astype(jnp.bfloat16)
        mg = jnp.dot(pg, wpool_ref[gi], preferred_element_type=jnp.float32)
        pooled.append((mg * pscale_ref[:, cols]).astype(jnp.bfloat16))
    pool = jnp.concatenate(pooled, axis=1)

    x1 = (x_ref[0]
          + lax.dot_general(at_ref[0], wout_ref[:ATTN_WIDTH, :], _TN,
                            preferred_element_type=jnp.float32)
          + jnp.dot(pool, wout_ref[ATTN_WIDTH:, :], preferred_element_type=jnp.float32))

    h2 = _rms(x1, g2_ref[...]).astype(jnp.bfloat16)
    ffn = jnp.zeros((tm, D_MODEL), jnp.float32)
    for c0 in range(0, D_FF, FF_CHUNK):
        gate = jnp.dot(h2, wg_ref[:, c0:c0 + FF_CHUNK], preferred_element_type=jnp.float32)
        up = jnp.dot(h2, wu_ref[:, c0:c0 + FF_CHUNK], preferred_element_type=jnp.float32)
        act = (gate / (1.0 + jnp.exp(-gate)) * up).astype(jnp.bfloat16)
        ffn = ffn + jnp.dot(act, wd_ref[c0:c0 + FF_CHUNK, :], preferred_element_type=jnp.float32)

    x2 = x1 + ffn
    y_ref[0] = _rms(x2, gf_ref[...]) if final_norm else x2


def _mix_ffn(x, attn_t, u, wpool, pscale, wout, g2, wg, wu, wd, gf, final_norm):
    B, S, D = x.shape
    tm = TOKEN_TILE
    halo_blocks = tm // POOL_HALO
    const = lambda shape: pl.BlockSpec(shape, lambda b, j: (0,) * len(shape),
                                       pipeline_mode=pl.Buffered(1))
    return pl.pallas_call(
        functools.partial(_mix_ffn_kernel, final_norm=final_norm),
        grid=(B, S // tm),
        in_specs=[
            pl.BlockSpec((1, tm, D), lambda b, j: (b, j, 0)),
            pl.BlockSpec((1, ATTN_WIDTH, tm), lambda b, j: (b, 0, j)),
            pl.BlockSpec((1, tm, POOL_WIDTH), lambda b, j: (b, j, 0)),
            pl.BlockSpec((1, POOL_HALO, POOL_WIDTH),
                         lambda b, j: (b, jnp.maximum(j * halo_blocks - 1, 0), 0)),
            const(wpool.shape), const(pscale.shape), const(wout.shape), const(g2.shape),
            const(wg.shape), const(wu.shape), const(wd.shape), const(gf.shape),
        ],
        out_specs=pl.BlockSpec((1, tm, D), lambda b, j: (b, j, 0)),
        out_shape=jax.ShapeDtypeStruct((B, S, D), jnp.float32),
        scratch_shapes=[pltpu.VMEM((tm + POOL_HALO, POOL_WIDTH), jnp.float32)],
        compiler_params=pltpu.CompilerParams(
            dimension_semantics=("arbitrary", "arbitrary"), vmem_limit_bytes=VMEM_LIMIT),
        name="mix_ffn",
    )(x, attn_t, u, u, wpool, pscale, wout, g2, wg, wu, wd, gf)


def kernel(x, norm1_g, w_in, b_forget, w_pool, pool_scale, w_out, norm2_g, w_gate, w_up, w_down, final_g):
    depth = w_in.shape[0]
    bf16 = jnp.bfloat16
    a0 = ATTN_WIDTH
    tri = jnp.tril(jnp.ones((TOKEN_TILE, TOKEN_TILE), bf16))
    lane = jnp.arange(LANES)
    bias_lane = lane < N_BIAS_TERMS * N_HEADS
    mod3 = jnp.where(bias_lane, lane % N_BIAS_TERMS, N_BIAS_TERMS).astype(jnp.int32)[None, :]
    head_of_lane = jnp.minimum(lane // N_BIAS_TERMS, N_HEADS - 1)
    for layer in range(depth):
        w = w_in[layer]
        wqk = w[:, :2 * a0].astype(bf16)
        wvt = w[:, 2 * a0:3 * a0].T.astype(bf16)
        wf = jnp.where(bias_lane[None, :], w[:, 3 * a0:3 * a0 + N_HEADS][:, head_of_lane], 0.0).astype(bf16)
        bf = jnp.where(bias_lane, b_forget[layer][head_of_lane], 0.0).astype(jnp.float32)[None, :]
        wu_in = w[:, 3 * a0 + N_HEADS:].astype(bf16)
        qp, kp, vtp, u = _in_proj(x, norm1_g[layer][None, :], wqk, wvt, wu_in, wf, bf, tri, mod3)
        attn_t = _fox_attn(qp, kp, vtp)
        x = _mix_ffn(x, attn_t, u, w_pool[layer].astype(bf16), pool_scale[layer][None, :],
                     w_out[layer].astype(bf16), norm2_g[layer][None, :], w_gate[layer].astype(bf16),
                     w_up[layer].astype(bf16), w_down[layer].astype(bf16), final_g[None, :],
                     final_norm=layer == depth - 1)
    return x
```

```python
import functools
import math

import jax
import jax.numpy as jnp
from jax import lax
from jax.experimental import pallas as pl
from jax.experimental.pallas import tpu as pltpu

D_MODEL = 1024
ATTN_WIDTH = 512
N_HEADS = 8
HEAD_DIM = 64
POOL_WIDTH = 512
POOL_WINDOWS = (2, 4, 8, 16)
POOL_GROUP = 128
D_FF = 2816
EPS = 1e-6

LANES = 128
HEAD_SLOT = 128
N_BIAS_TERMS = 3
VT_ROWS = HEAD_DIM + 16
POOL_HALO = 16
POOL_PAD = 8
LOG2E = math.log2(math.e)
NEG_BIG = -1e30

TOKEN_TILE = 512
ATTN_BLOCK = 512
ATTN_UNIT = 256
ATTN_HEADS_PER_STEP = 4
ATTN_ISSUE_AHEAD = 2
FF_CHUNK = 1408
VMEM_LIMIT = 56 * 1024 * 1024

_NT = (((1,), (1,)), ((), ()))
_TN = (((0,), (0,)), ((), ()))


def _rms(x, g):
    ms = jnp.mean(x * x, axis=-1, keepdims=True)
    return x * lax.rsqrt(ms + EPS) * g


def _split_bf16(x):
    hi = x.astype(jnp.bfloat16)
    r1 = x - hi.astype(jnp.float32)
    mid = r1.astype(jnp.bfloat16)
    lo = (r1 - mid.astype(jnp.float32)).astype(jnp.bfloat16)
    return hi, mid, lo


def _in_proj_kernel(x_ref, g_ref, wqk_ref, wvt_ref, wu_ref, wf_ref, bf_ref, tri_ref, mod3_ref,
                    qp_ref, kp_ref, vt_ref, u_ref, carry_ref):
    tm = x_ref.shape[1]

    @pl.when(pl.program_id(1) == 0)
    def _():
        carry_ref[...] = jnp.zeros_like(carry_ref)

    h = _rms(x_ref[0], g_ref[...]).astype(jnp.bfloat16)
    f = jnp.dot(h, wf_ref[...], preferred_element_type=jnp.float32) + bf_ref[...]
    u_ref[0] = jnp.dot(h, wu_ref[...], preferred_element_type=jnp.float32)

    lf = jnp.minimum(f, 0.0) - jnp.log(1.0 + jnp.exp(-jnp.abs(f)))
    pieces = jnp.concatenate(_split_bf16(lf), axis=1)
    vt = lax.dot_general(wvt_ref[...], h, _NT, preferred_element_type=jnp.float32)
    cs3 = jnp.dot(tri_ref[...], pieces, preferred_element_type=jnp.float32)
    qk = jnp.dot(h, wqk_ref[...], preferred_element_type=jnp.float32)
    c = cs3[:, :LANES] + cs3[:, LANES:2 * LANES] + cs3[:, 2 * LANES:] + carry_ref[...]
    carry_ref[...] = c[tm - 1:tm, :]

    hi, mid, lo = _split_bf16(c * (-LOG2E))
    m3 = mod3_ref[...]
    e3 = jnp.where(m3 == 0, hi.astype(jnp.float32),
                   jnp.where(m3 == 1, mid.astype(jnp.float32), lo.astype(jnp.float32)))

    lane = lax.broadcasted_iota(jnp.int32, (tm, LANES), 1)
    low_half = lane < HEAD_DIM
    ones_hi = jnp.where((lane >= HEAD_DIM) & (lane < HEAD_DIM + N_BIAS_TERMS), 1.0, 0.0)
    ones_lo = jnp.where(lane < N_BIAS_TERMS, 1.0, 0.0)
    q_scale = HEAD_DIM ** -0.5 * LOG2E
    for hd in range(N_HEADS):
        pair = hd // 2
        qpair = qk[:, pair * LANES:(pair + 1) * LANES] * q_scale
        kpair = qk[:, ATTN_WIDTH + pair * LANES:ATTN_WIDTH + (pair + 1) * LANES]
        if hd % 2 == 0:
            bias = pltpu.roll(e3, HEAD_DIM - N_BIAS_TERMS * hd, 1)
            qh = jnp.where(low_half, qpair, ones_hi)
            kh = jnp.where(low_half, kpair, bias)
        else:
            bias = pltpu.roll(e3, LANES - N_BIAS_TERMS * hd, 1)
            qh = jnp.where(low_half, ones_lo, qpair)
            kh = jnp.where(low_half, bias, kpair)
        qp_ref[0, :, hd * HEAD_SLOT:(hd + 1) * HEAD_SLOT] = qh.astype(jnp.bfloat16)
        kp_ref[0, :, hd * HEAD_SLOT:(hd + 1) * HEAD_SLOT] = kh.astype(jnp.bfloat16)
        vt_ref[0, hd * VT_ROWS:hd * VT_ROWS + HEAD_DIM, :] = (
            vt[hd * HEAD_DIM:(hd + 1) * HEAD_DIM, :].astype(jnp.bfloat16))
        vt_ref[0, hd * VT_ROWS + HEAD_DIM:(hd + 1) * VT_ROWS, :] = jnp.ones(
            (VT_ROWS - HEAD_DIM, tm), jnp.bfloat16)


def _in_proj(x, g, wqk, wvt, wu, wf, bf, tri, mod3):
    B, S, D = x.shape
    tm = TOKEN_TILE
    const = lambda shape: pl.BlockSpec(shape, lambda b, j: (0,) * len(shape))
    return pl.pallas_call(
        _in_proj_kernel,
        grid=(B, S // tm),
        in_specs=[
            pl.BlockSpec((1, tm, D), lambda b, j: (b, j, 0)),
            const(g.shape), const(wqk.shape), const(wvt.shape), const(wu.shape),
            const(wf.shape), const(bf.shape), const(tri.shape), const(mod3.shape),
        ],
        out_specs=[
            pl.BlockSpec((1, tm, N_HEADS * HEAD_SLOT), lambda b, j: (b, j, 0)),
            pl.BlockSpec((1, tm, N_HEADS * HEAD_SLOT), lambda b, j: (b, j, 0)),
            pl.BlockSpec((1, N_HEADS * VT_ROWS, tm), lambda b, j: (b, 0, j)),
            pl.BlockSpec((1, tm, POOL_WIDTH), lambda b, j: (b, j, 0)),
        ],
        out_shape=[
            jax.ShapeDtypeStruct((B, S, N_HEADS * HEAD_SLOT), jnp.bfloat16),
            jax.ShapeDtypeStruct((B, S, N_HEADS * HEAD_SLOT), jnp.bfloat16),
            jax.ShapeDtypeStruct((B, N_HEADS * VT_ROWS, S), jnp.bfloat16),
            jax.ShapeDtypeStruct((B, S, POOL_WIDTH), jnp.float32),
        ],
        scratch_shapes=[pltpu.VMEM((1, LANES), jnp.float32)],
        compiler_params=pltpu.CompilerParams(
            dimension_semantics=("arbitrary", "arbitrary"), vmem_limit_bytes=VMEM_LIMIT),
        name="in_proj",
    )(x, g, wqk, wvt, wu, wf, bf, tri, mod3)


def _fox_attn_kernel(q_ref, k_ref, vt_ref, o_ref, m_ref, acc_ref, pend_ref):
    S = q_ref.shape[1]
    blk, sub = ATTN_BLOCK, ATTN_UNIT
    heads = q_ref.shape[2] // HEAD_SLOT
    groups = blk // sub
    n_q = S // blk
    tiles = [(ks, hd, qg) for ks in range(groups) for hd in range(heads) for qg in range(groups)]

    def scores(hd, qg, q0, k0):
        lanes = slice(hd * HEAD_SLOT, (hd + 1) * HEAD_SLOT)
        qt = q_ref[0, pl.ds(q0 + qg * sub, sub), lanes]
        kt = k_ref[0, pl.ds(k0, sub), lanes]
        return lax.dot_general(kt, qt, _NT, preferred_element_type=jnp.float32)

    def update(hd, k0, st, m_old, acc_old):
        m_new = jnp.maximum(m_old, jnp.max(st, axis=0, keepdims=True))
        p = jnp.exp2(st - m_new).astype(jnp.bfloat16)
        alpha = jnp.exp2(m_old - m_new)
        vt = vt_ref[0, hd * VT_ROWS:(hd + 1) * VT_ROWS, pl.ds(k0, sub)]
        return m_new, alpha * acc_old + jnp.dot(vt, p, preferred_element_type=jnp.float32)

    def step(slot, k0, nxt_q0, nxt_k0, diag):
        for static_slot in range(2):
            @pl.when(slot == static_slot)
            def _():
                static_step(static_slot, k0, nxt_q0, nxt_k0, diag)

    def static_step(slot, k0, nxt_q0, nxt_k0, diag):
        state = {(hd, qg): (m_ref[hd, :, qg * sub:(qg + 1) * sub],
                            acc_ref[hd, :, qg * sub:(qg + 1) * sub])
                 for hd in range(heads) for qg in range(groups)}
        def issue(n):
            ks, hd, qg = tiles[n]
            pend_ref[1 - slot, n] = scores(hd, qg, nxt_q0, nxt_k0 + ks * sub)

        for n in range(min(ATTN_ISSUE_AHEAD, len(tiles))):
            issue(n)
        for n, (ks, hd, qg) in enumerate(tiles):
            if n + ATTN_ISSUE_AHEAD < len(tiles):
                issue(n + ATTN_ISSUE_AHEAD)
            if diag and qg < ks:
                continue
            st = pend_ref[slot, n]
            if diag and qg == ks:
                key = lax.broadcasted_iota(jnp.int32, (sub, sub), 0)
                qry = lax.broadcasted_iota(jnp.int32, (sub, sub), 1)
                st = jnp.where(key <= qry, st, NEG_BIG)
            state[hd, qg] = update(hd, k0 + ks * sub, st, *state[hd, qg])
        for (hd, qg), (m, acc) in state.items():
            m_ref[hd, :, qg * sub:(qg + 1) * sub] = m
            acc_ref[hd, :, qg * sub:(qg + 1) * sub] = acc

    def q_block(j, slot):
        q0 = pl.multiple_of(j * blk, blk)
        m_ref[...] = jnp.full(m_ref.shape, NEG_BIG, jnp.float32)
        acc_ref[...] = jnp.zeros(acc_ref.shape, jnp.float32)

        def k_block(i, slot):
            k0 = pl.multiple_of(i * blk, blk)
            step(slot, k0, q0, k0 + blk, diag=False)
            return 1 - slot

        slot = lax.fori_loop(0, j, k_block, slot)
        nxt_q0 = pl.multiple_of(jnp.minimum(j + 1, n_q - 1) * blk, blk)
        step(slot, q0, nxt_q0, 0, diag=True)

        for hd in range(heads):
            acc = acc_ref[hd]
            out = acc[:HEAD_DIM, :] / acc[HEAD_DIM:HEAD_DIM + 1, :]
            o_ref[0, hd * HEAD_DIM:(hd + 1) * HEAD_DIM, pl.ds(q0, blk)] = out.astype(o_ref.dtype)
        return 1 - slot

    for n, (ks, hd, qg) in enumerate(tiles):
        pend_ref[0, n] = scores(hd, qg, 0, ks * sub)
    lax.fori_loop(0, n_q, q_block, 0)


def _fox_attn(qp, kp, vtp):
    B, S, _ = qp.shape
    hp = ATTN_HEADS_PER_STEP
    return pl.pallas_call(
        _fox_attn_kernel,
        grid=(B, N_HEADS // hp),
        in_specs=[
            pl.BlockSpec((1, S, hp * HEAD_SLOT), lambda b, h: (b, 0, h)),
            pl.BlockSpec((1, S, hp * HEAD_SLOT), lambda b, h: (b, 0, h)),
            pl.BlockSpec((1, hp * VT_ROWS, S), lambda b, h: (b, h, 0)),
        ],
        out_specs=pl.BlockSpec((1, hp * HEAD_DIM, S), lambda b, h: (b, h, 0)),
        out_shape=jax.ShapeDtypeStruct((B, ATTN_WIDTH, S), jnp.bfloat16),
        scratch_shapes=[pltpu.VMEM((hp, 1, ATTN_BLOCK), jnp.float32),
                        pltpu.VMEM((hp, VT_ROWS, ATTN_BLOCK), jnp.float32),
                        pltpu.VMEM((2, hp * (ATTN_BLOCK // ATTN_UNIT) ** 2, ATTN_UNIT, ATTN_UNIT),
                                   jnp.float32)],
        compiler_params=pltpu.CompilerParams(
            dimension_semantics=("arbitrary", "arbitrary"), vmem_limit_bytes=VMEM_LIMIT),
        name="fox_attn",
    )(qp, kp, vtp)


def _mix_ffn_kernel(x_ref, at_ref, u_ref, uprev_ref, wpool_ref, pscale_ref, wout_ref, g2_ref,
                    wg_ref, wu_ref, wd_ref, gf_ref, y_ref, ext_ref, stage_ref, *, final_norm):
    tm = x_ref.shape[1]
    j = pl.program_id(1)

    x1 = x_ref[0] + lax.dot_general(at_ref[0], wout_ref[:ATTN_WIDTH, :], _TN,
                                    preferred_element_type=jnp.float32)

    pad, top = POOL_PAD, POOL_PAD + POOL_HALO
    u = u_ref[0]
    ext_ref[:pad, :] = jnp.zeros((pad, POOL_WIDTH), jnp.float32)
    ext_ref[pad:top, :] = jnp.where(j > 0, uprev_ref[0], 0.0)
    ext_ref[top:, :] = u
    stage_ref[:, :pad, :] = jnp.zeros((stage_ref.shape[0], pad, POOL_GROUP), jnp.float32)

    first_rows = lax.broadcasted_iota(jnp.int32, (POOL_HALO, POOL_GROUP), 0) + 1
    pooled = []
    for gi, w in enumerate(POOL_WINDOWS):
        cols = slice(gi * POOL_GROUP, (gi + 1) * POOL_GROUP)
        end = ext_ref.shape[0]
        wsum = ext_ref[pad:end, cols] + ext_ref[pad - 1:end - 1, cols]
        span = 2
        while span < w:
            level = stage_ref.at[(span.bit_length() - 2) % stage_ref.shape[0]]
            level[pad:end, :] = wsum
            wsum = wsum + level[pad - span:end - span, :]
            span *= 2
        wsum = wsum[POOL_HALO:, :]
        mean = wsum * (1.0 / w)
        exact_head = wsum[:POOL_HALO, :] / jnp.minimum(first_rows, w).astype(jnp.float32)
        head = jnp.where(j == 0, exact_head, mean[:POOL_HALO, :])
        mean = jnp.concatenate([head, mean[POOL_HALO:, :]], axis=0)
        pooled.append((mean - u[:, cols]).astype(jnp.bfloat16))
    mixed = []
    for gi in range(len(POOL_WINDOWS)):
        cols = slice(gi * POOL_GROUP, (gi + 1) * POOL_GROUP)
        mg = jnp.dot(pooled[gi], wpool_ref[gi], preferred_element_type=jnp.float32)
        mixed.append((mg * pscale_ref[:, cols]).astype(jnp.bfloat16))
    pool = jnp.concatenate(mixed, axis=1)
    x1 = x1 + jnp.dot(pool, wout_ref[ATTN_WIDTH:, :], preferred_element_type=jnp.float32)

    h2 = _rms(x1, g2_ref[...]).astype(jnp.bfloat16)
    ffn = jnp.zeros((tm, D_MODEL), jnp.float32)
    for c0 in range(0, D_FF, FF_CHUNK):
        gate = jnp.dot(h2, wg_ref[:, c0:c0 + FF_CHUNK], preferred_element_type=jnp.float32)
        up = jnp.dot(h2, wu_ref[:, c0:c0 + FF_CHUNK], preferred_element_type=jnp.float32)
        act = (gate / (1.0 + jnp.exp(-gate)) * up).astype(jnp.bfloat16)
        ffn = ffn + jnp.dot(act, wd_ref[c0:c0 + FF_CHUNK, :], preferred_element_type=jnp.float32)

    x2 = x1 + ffn
    y_ref[0] = _rms(x2, gf_ref[...]) if final_norm else x2


def _mix_ffn(x, attn_t, u, wpool, pscale, wout, g2, wg, wu, wd, gf, final_norm):
    B, S, D = x.shape
    tm = TOKEN_TILE
    halo_blocks = tm // POOL_HALO
    const = lambda shape: pl.BlockSpec(shape, lambda b, j: (0,) * len(shape),
                                       pipeline_mode=pl.Buffered(1))
    return pl.pallas_call(
        functools.partial(_mix_ffn_kernel, final_norm=final_norm),
        grid=(B, S // tm),
        in_specs=[
            pl.BlockSpec((1, tm, D), lambda b, j: (b, j, 0)),
            pl.BlockSpec((1, ATTN_WIDTH, tm), lambda b, j: (b, 0, j)),
            pl.BlockSpec((1, tm, POOL_WIDTH), lambda b, j: (b, j, 0)),
            pl.BlockSpec((1, POOL_HALO, POOL_WIDTH),
                         lambda b, j: (b, jnp.maximum(j * halo_blocks - 1, 0), 0)),
            const(wpool.shape), const(pscale.shape), const(wout.shape), const(g2.shape),
            const(wg.shape), const(wu.shape), const(wd.shape), const(gf.shape),
        ],
        out_specs=pl.BlockSpec((1, tm, D), lambda b, j: (b, j, 0)),
        out_shape=jax.ShapeDtypeStruct((B, S, D), jnp.float32),
        scratch_shapes=[pltpu.VMEM((POOL_PAD + POOL_HALO + tm, POOL_WIDTH), jnp.float32),
                        pltpu.VMEM((2, POOL_PAD + POOL_HALO + tm, POOL_GROUP), jnp.float32)],
        compiler_params=pltpu.CompilerParams(
            dimension_semantics=("arbitrary", "arbitrary"), vmem_limit_bytes=VMEM_LIMIT),
        name="mix_ffn",
    )(x, attn_t, u, u, wpool, pscale, wout, g2, wg, wu, wd, gf)


def kernel(x, norm1_g, w_in, b_forget, w_pool, pool_scale, w_out, norm2_g, w_gate, w_up, w_down, final_g):
    depth = w_in.shape[0]
    bf16 = jnp.bfloat16
    a0 = ATTN_WIDTH
    tri = jnp.tril(jnp.ones((TOKEN_TILE, TOKEN_TILE), bf16))
    lane = jnp.arange(LANES)
    bias_lane = lane < N_BIAS_TERMS * N_HEADS
    mod3 = jnp.where(bias_lane, lane % N_BIAS_TERMS, N_BIAS_TERMS).astype(jnp.int32)[None, :]
    head_of_lane = jnp.minimum(lane // N_BIAS_TERMS, N_HEADS - 1)
    for layer in range(depth):
        w = w_in[layer]
        wqk = w[:, :2 * a0].astype(bf16)
        wvt = w[:, 2 * a0:3 * a0].T.astype(bf16)
        wf = jnp.where(bias_lane[None, :], w[:, 3 * a0:3 * a0 + N_HEADS][:, head_of_lane], 0.0).astype(bf16)
        bf = jnp.where(bias_lane, b_forget[layer][head_of_lane], 0.0).astype(jnp.float32)[None, :]
        wu_in = w[:, 3 * a0 + N_HEADS:].astype(bf16)
        qp, kp, vtp, u = _in_proj(x, norm1_g[layer][None, :], wqk, wvt, wu_in, wf, bf, tri, mod3)
        attn_t = _fox_attn(qp, kp, vtp)
        x = _mix_ffn(x, attn_t, u, w_pool[layer].astype(bf16), pool_scale[layer][None, :],
                     w_out[layer].astype(bf16), norm2_g[layer][None, :], w_gate[layer].astype(bf16),
                     w_up[layer].astype(bf16), w_down[layer].astype(bf16), final_g[None, :],
                     final_norm=layer == depth - 1)
    return x
```

```python
import functools
import math

import jax
import jax.numpy as jnp
import numpy as np
from jax import lax
from jax.experimental import pallas as pl
from jax.experimental.pallas import tpu as pltpu

D_MODEL = 1024
ATTN_WIDTH = 512
N_HEADS = 8
HEAD_DIM = 64
POOL_WIDTH = 512
POOL_WINDOWS = (2, 4, 8, 16)
POOL_GROUP = 128
D_FF = 2816
EPS = 1e-6

LANES = 128
HEAD_SLOT = 128
N_BIAS_TERMS = 3
VT_ROWS = HEAD_DIM + 16
POOL_HALO = 16
POOL_PAD = 8
LOG2E = math.log2(math.e)
NEG_BIG = -1e30

TOKEN_TILE = 512
ATTN_BLOCK = 512
ATTN_UNIT = 256
ATTN_HEADS_PER_STEP = 4
ATTN_ISSUE_AHEAD = 2
FF_CHUNK = 1408
VMEM_LIMIT = 56 * 1024 * 1024

_NT = (((1,), (1,)), ((), ()))
_TN = (((0,), (0,)), ((), ()))


def _rms(x, g):
    ms = jnp.mean(x * x, axis=-1, keepdims=True)
    return x * lax.rsqrt(ms + EPS) * g


def _split_bf16(x):
    hi = x.astype(jnp.bfloat16)
    r1 = x - hi.astype(jnp.float32)
    mid = r1.astype(jnp.bfloat16)
    lo = (r1 - mid.astype(jnp.float32)).astype(jnp.bfloat16)
    return hi, mid, lo


def _in_proj_kernel(x_ref, g_ref, wqk_ref, wvt_ref, wuf_ref, bf_ref, mod3_ref,
                    qp_ref, kp_ref, vt_ref, u_ref, carry_ref, scan_ref):
    tm = x_ref.shape[1]
    pad = scan_ref.shape[1] - tm

    @pl.when(pl.program_id(1) == 0)
    def _():
        carry_ref[...] = jnp.zeros_like(carry_ref)

    h = _rms(x_ref[0], g_ref[...]).astype(jnp.bfloat16)
    uf = jnp.dot(h, wuf_ref[...], preferred_element_type=jnp.float32)
    u_ref[0] = uf[:, :POOL_WIDTH]
    f = uf[:, POOL_WIDTH:] + bf_ref[...]
    vt = lax.dot_general(wvt_ref[...], h, _NT, preferred_element_type=jnp.float32)
    qk = jnp.dot(h, wqk_ref[...], preferred_element_type=jnp.float32)

    lf = jnp.minimum(f, 0.0) - jnp.log(1.0 + jnp.exp(-jnp.abs(f)))
    scan_ref[:, :pad, :] = jnp.zeros((scan_ref.shape[0], pad, LANES), jnp.float32)
    shift, level = 1, 0
    while shift < tm:
        scan_ref[level, pad:, :] = lf
        lf = lf + scan_ref[level, pad - shift:pad - shift + tm, :]
        shift, level = 2 * shift, 1 - level
    c = lf + carry_ref[...]
    carry_ref[...] = c[tm - 1:tm, :]

    hi, mid, lo = _split_bf16(c * (-LOG2E))
    m3 = mod3_ref[...]
    e3 = jnp.where(m3 == 0, hi.astype(jnp.float32),
                   jnp.where(m3 == 1, mid.astype(jnp.float32), lo.astype(jnp.float32)))

    lane = lax.broadcasted_iota(jnp.int32, (tm, LANES), 1)
    low_half = lane < HEAD_DIM
    ones_hi = jnp.where((lane >= HEAD_DIM) & (lane < HEAD_DIM + N_BIAS_TERMS), 1.0, 0.0)
    ones_lo = jnp.where(lane < N_BIAS_TERMS, 1.0, 0.0)
    q_scale = HEAD_DIM ** -0.5 * LOG2E
    for hd in range(N_HEADS):
        pair = hd // 2
        qpair = qk[:, pair * LANES:(pair + 1) * LANES] * q_scale
        kpair = qk[:, ATTN_WIDTH + pair * LANES:ATTN_WIDTH + (pair + 1) * LANES]
        if hd % 2 == 0:
            bias = pltpu.roll(e3, HEAD_DIM - N_BIAS_TERMS * hd, 1)
            qh = jnp.where(low_half, qpair, ones_hi)
            kh = jnp.where(low_half, kpair, bias)
        else:
            bias = pltpu.roll(e3, LANES - N_BIAS_TERMS * hd, 1)
            qh = jnp.where(low_half, ones_lo, qpair)
            kh = jnp.where(low_half, bias, kpair)
        qp_ref[0, :, hd * HEAD_SLOT:(hd + 1) * HEAD_SLOT] = qh.astype(jnp.bfloat16)
        kp_ref[0, :, hd * HEAD_SLOT:(hd + 1) * HEAD_SLOT] = kh.astype(jnp.bfloat16)
        vt_ref[0, hd * VT_ROWS:hd * VT_ROWS + HEAD_DIM, :] = (
            vt[hd * HEAD_DIM:(hd + 1) * HEAD_DIM, :].astype(jnp.bfloat16))
        vt_ref[0, hd * VT_ROWS + HEAD_DIM:(hd + 1) * VT_ROWS, :] = jnp.ones(
            (VT_ROWS - HEAD_DIM, tm), jnp.bfloat16)


def _in_proj(x, g, wqk, wvt, wuf, bf, mod3):
    B, S, D = x.shape
    tm = TOKEN_TILE
    const = lambda shape: pl.BlockSpec(shape, lambda b, j: (0,) * len(shape))
    return pl.pallas_call(
        _in_proj_kernel,
        grid=(B, S // tm),
        in_specs=[
            pl.BlockSpec((1, tm, D), lambda b, j: (b, j, 0)),
            const(g.shape), const(wqk.shape), const(wvt.shape), const(wuf.shape),
            const(bf.shape), const(mod3.shape),
        ],
        out_specs=[
            pl.BlockSpec((1, tm, N_HEADS * HEAD_SLOT), lambda b, j: (b, j, 0)),
            pl.BlockSpec((1, tm, N_HEADS * HEAD_SLOT), lambda b, j: (b, j, 0)),
            pl.BlockSpec((1, N_HEADS * VT_ROWS, tm), lambda b, j: (b, 0, j)),
            pl.BlockSpec((1, tm, POOL_WIDTH), lambda b, j: (b, j, 0)),
        ],
        out_shape=[
            jax.ShapeDtypeStruct((B, S, N_HEADS * HEAD_SLOT), jnp.bfloat16),
            jax.ShapeDtypeStruct((B, S, N_HEADS * HEAD_SLOT), jnp.bfloat16),
            jax.ShapeDtypeStruct((B, N_HEADS * VT_ROWS, S), jnp.bfloat16),
            jax.ShapeDtypeStruct((B, S, POOL_WIDTH), jnp.float32),
        ],
        scratch_shapes=[pltpu.VMEM((1, LANES), jnp.float32),
                        pltpu.VMEM((2, tm // 2 + tm, LANES), jnp.float32)],
        compiler_params=pltpu.CompilerParams(
            dimension_semantics=("arbitrary", "arbitrary"), vmem_limit_bytes=VMEM_LIMIT),
        name="in_proj",
    )(x, g, wqk, wvt, wuf, bf, mod3)


def _fox_attn_kernel(q_ref, k_ref, vt_ref, o_ref, m_ref, acc_ref, pend_ref):
    S = q_ref.shape[1]
    blk, sub = ATTN_BLOCK, ATTN_UNIT
    heads = q_ref.shape[2] // HEAD_SLOT
    groups = blk // sub
    n_q = S // blk
    tiles = [(ks, hd, qg) for ks in range(groups) for hd in range(heads) for qg in range(groups)]

    def scores(hd, qg, q0, k0):
        lanes = slice(hd * HEAD_SLOT, (hd + 1) * HEAD_SLOT)
        qt = q_ref[0, pl.ds(q0 + qg * sub, sub), lanes]
        kt = k_ref[0, pl.ds(k0, sub), lanes]
        return lax.dot_general(kt, qt, _NT, preferred_element_type=jnp.float32)

    def update(hd, k0, st, m_old, acc_old):
        m_new = jnp.maximum(m_old, jnp.max(st, axis=0, keepdims=True))
        p = jnp.exp2(st - m_new).astype(jnp.bfloat16)
        alpha = jnp.exp2(m_old - m_new)
        vt = vt_ref[0, hd * VT_ROWS:(hd + 1) * VT_ROWS, pl.ds(k0, sub)]
        return m_new, alpha * acc_old + jnp.dot(vt, p, preferred_element_type=jnp.float32)

    def step(slot, k0, nxt_q0, nxt_k0, diag):
        for static_slot in range(2):
            @pl.when(slot == static_slot)
            def _():
                static_step(static_slot, k0, nxt_q0, nxt_k0, diag)

    def static_step(slot, k0, nxt_q0, nxt_k0, diag):
        state = {(hd, qg): (m_ref[hd, :, qg * sub:(qg + 1) * sub],
                            acc_ref[hd, :, qg * sub:(qg + 1) * sub])
                 for hd in range(heads) for qg in range(groups)}
        def issue(n):
            ks, hd, qg = tiles[n]
            pend_ref[1 - slot, n] = scores(hd, qg, nxt_q0, nxt_k0 + ks * sub)

        for n in range(min(ATTN_ISSUE_AHEAD, len(tiles))):
            issue(n)
        for n, (ks, hd, qg) in enumerate(tiles):
            if n + ATTN_ISSUE_AHEAD < len(tiles):
                issue(n + ATTN_ISSUE_AHEAD)
            if diag and qg < ks:
                continue
            st = pend_ref[slot, n]
            if diag and qg == ks:
                key = lax.broadcasted_iota(jnp.int32, (sub, sub), 0)
                qry = lax.broadcasted_iota(jnp.int32, (sub, sub), 1)
                st = jnp.where(key <= qry, st, NEG_BIG)
            state[hd, qg] = update(hd, k0 + ks * sub, st, *state[hd, qg])
        for (hd, qg), (m, acc) in state.items():
            m_ref[hd, :, qg * sub:(qg + 1) * sub] = m
            acc_ref[hd, :, qg * sub:(qg + 1) * sub] = acc

    def q_block(j, slot):
        q0 = pl.multiple_of(j * blk, blk)
        m_ref[...] = jnp.full(m_ref.shape, NEG_BIG, jnp.float32)
        acc_ref[...] = jnp.zeros(acc_ref.shape, jnp.float32)

        def k_block(i, slot):
            k0 = pl.multiple_of(i * blk, blk)
            step(slot, k0, q0, k0 + blk, diag=False)
            return 1 - slot

        slot = lax.fori_loop(0, j, k_block, slot)
        nxt_q0 = pl.multiple_of(jnp.minimum(j + 1, n_q - 1) * blk, blk)
        step(slot, q0, nxt_q0, 0, diag=True)

        for hd in range(heads):
            acc = acc_ref[hd]
            out = acc[:HEAD_DIM, :] / acc[HEAD_DIM:HEAD_DIM + 1, :]
            o_ref[0, hd * HEAD_DIM:(hd + 1) * HEAD_DIM, pl.ds(q0, blk)] = out.astype(o_ref.dtype)
        return 1 - slot

    for n, (ks, hd, qg) in enumerate(tiles):
        pend_ref[0, n] = scores(hd, qg, 0, ks * sub)
    lax.fori_loop(0, n_q, q_block, 0)


def _fox_attn(qp, kp, vtp):
    B, S, _ = qp.shape
    hp = ATTN_HEADS_PER_STEP
    return pl.pallas_call(
        _fox_attn_kernel,
        grid=(B, N_HEADS // hp),
        in_specs=[
            pl.BlockSpec((1, S, hp * HEAD_SLOT), lambda b, h: (b, 0, h)),
            pl.BlockSpec((1, S, hp * HEAD_SLOT), lambda b, h: (b, 0, h)),
            pl.BlockSpec((1, hp * VT_ROWS, S), lambda b, h: (b, h, 0)),
        ],
        out_specs=pl.BlockSpec((1, hp * HEAD_DIM, S), lambda b, h: (b, h, 0)),
        out_shape=jax.ShapeDtypeStruct((B, ATTN_WIDTH, S), jnp.bfloat16),
        scratch_shapes=[pltpu.VMEM((hp, 1, ATTN_BLOCK), jnp.float32),
                        pltpu.VMEM((hp, VT_ROWS, ATTN_BLOCK), jnp.float32),
                        pltpu.VMEM((2, hp * (ATTN_BLOCK // ATTN_UNIT) ** 2, ATTN_UNIT, ATTN_UNIT),
                                   jnp.float32)],
        compiler_params=pltpu.CompilerParams(
            dimension_semantics=("arbitrary", "arbitrary"), vmem_limit_bytes=VMEM_LIMIT),
        name="fox_attn",
    )(qp, kp, vtp)


def _mix_ffn_kernel(x_ref, at_ref, u_ref, uprev_ref, wpool_ref, pscale_ref, wout_ref, g2_ref,
                    wg_ref, wu_ref, wd_ref, gf_ref, y_ref, ext_ref, stage_ref, x1_ref, h2_ref,
                    *, final_norm, n_tiles, tiles_per_seq):
    s = pl.program_id(0)

    @pl.when(s == 0)
    def _():
        x1_ref[1] = jnp.zeros(x1_ref.shape[1:], x1_ref.dtype)
        h2_ref[1] = jnp.zeros(h2_ref.shape[1:], h2_ref.dtype)

    args = (x_ref, at_ref, u_ref, uprev_ref, wpool_ref, pscale_ref, wout_ref, g2_ref,
            wg_ref, wu_ref, wd_ref, gf_ref, y_ref, ext_ref, stage_ref, x1_ref, h2_ref)
    j = lax.rem(jnp.minimum(s, n_tiles - 1), tiles_per_seq)
    for slot in range(2):
        @pl.when(lax.rem(s, 2) == slot)
        def _():
            _mix_ffn_step(*args, j=j, slot=slot, final_norm=final_norm)


def _mix_ffn_step(x_ref, at_ref, u_ref, uprev_ref, wpool_ref, pscale_ref, wout_ref, g2_ref,
                  wg_ref, wu_ref, wd_ref, gf_ref, y_ref, ext_ref, stage_ref, x1_ref, h2_ref,
                  *, j, slot, final_norm):
    tm = x_ref.shape[1]
    chunks = list(range(0, D_FF, FF_CHUNK))

    def ffn_chunk(h2, c0, rows=slice(None)):
        gate = jnp.dot(h2, wg_ref[:, c0:c0 + FF_CHUNK], preferred_element_type=jnp.float32)
        up = jnp.dot(h2, wu_ref[:, c0:c0 + FF_CHUNK], preferred_element_type=jnp.float32)
        return (gate / (1.0 + jnp.exp(-gate)) * up).astype(jnp.bfloat16)

    h2_prev = h2_ref[1 - slot]
    ffn = jnp.zeros((tm, D_MODEL), jnp.float32)
    for c0 in chunks[:-1]:
        ffn = ffn + jnp.dot(ffn_chunk(h2_prev, c0), wd_ref[c0:c0 + FF_CHUNK, :],
                            preferred_element_type=jnp.float32)
    act_last = ffn_chunk(h2_prev, chunks[-1])

    x1 = x_ref[0] + lax.dot_general(at_ref[0], wout_ref[:ATTN_WIDTH, :], _TN,
                                    preferred_element_type=jnp.float32)

    pad, top = POOL_PAD, POOL_PAD + POOL_HALO
    u = u_ref[0]
    ext_ref[:pad, :] = jnp.zeros((pad, POOL_WIDTH), jnp.float32)
    ext_ref[pad:top, :] = jnp.where(j > 0, uprev_ref[0], 0.0)
    ext_ref[top:, :] = u
    stage_ref[:, :pad, :] = jnp.zeros((stage_ref.shape[0], pad, POOL_GROUP), jnp.float32)

    first_rows = lax.broadcasted_iota(jnp.int32, (POOL_HALO, POOL_GROUP), 0) + 1
    pooled = []
    for gi, w in enumerate(POOL_WINDOWS):
        cols = slice(gi * POOL_GROUP, (gi + 1) * POOL_GROUP)
        end = ext_ref.shape[0]
        wsum = ext_ref[pad:end, cols] + ext_ref[pad - 1:end - 1, cols]
        span = 2
        while span < w:
            level = stage_ref.at[(span.bit_length() - 2) % stage_ref.shape[0]]
            level[pad:end, :] = wsum
            wsum = wsum + level[pad - span:end - span, :]
            span *= 2
        wsum = wsum[POOL_HALO:, :]
        mean = wsum * (1.0 / w)
        exact_head = wsum[:POOL_HALO, :] / jnp.minimum(first_rows, w).astype(jnp.float32)
        head = jnp.where(j == 0, exact_head, mean[:POOL_HALO, :])
        mean = jnp.concatenate([head, mean[POOL_HALO:, :]], axis=0)
        pooled.append((mean - u[:, cols]).astype(jnp.bfloat16))
    mixed = []
    for gi in range(len(POOL_WINDOWS)):
        cols = slice(gi * POOL_GROUP, (gi + 1) * POOL_GROUP)
        mg = jnp.dot(pooled[gi], wpool_ref[gi], preferred_element_type=jnp.float32)
        mixed.append((mg * pscale_ref[:, cols]).astype(jnp.bfloat16))
    pool = jnp.concatenate(mixed, axis=1)
    x1 = x1 + jnp.dot(pool, wout_ref[ATTN_WIDTH:, :], preferred_element_type=jnp.float32)
    x1_ref[slot] = x1
    h2_ref[slot] = _rms(x1, g2_ref[...]).astype(jnp.bfloat16)

    half = tm // 2
    c0 = chunks[-1]
    for r0 in (0, half):
        rows = slice(r0, r0 + half)
        x2 = (x1_ref[1 - slot, rows, :] + ffn[rows, :]
              + jnp.dot(act_last[rows, :], wd_ref[c0:c0 + FF_CHUNK, :],
                        preferred_element_type=jnp.float32))
        y_ref[0, rows, :] = _rms(x2, gf_ref[...]) if final_norm else x2


def _mix_ffn(x, attn_t, u, wpool, pscale, wout, g2, wg, wu, wd, gf, final_norm):
    B, S, D = x.shape
    tm = TOKEN_TILE
    nj = S // tm
    n_tiles = B * nj
    halo_blocks = tm // POOL_HALO
    const = lambda shape: pl.BlockSpec(shape, lambda s: (0,) * len(shape),
                                       pipeline_mode=pl.Buffered(1))
    front = lambda s: jnp.minimum(s, n_tiles - 1)
    back = lambda s: jnp.maximum(s - 1, 0)
    return pl.pallas_call(
        functools.partial(_mix_ffn_kernel, final_norm=final_norm, n_tiles=n_tiles,
                          tiles_per_seq=nj),
        grid=(n_tiles + 1,),
        in_specs=[
            pl.BlockSpec((1, tm, D), lambda s: (front(s) // nj, front(s) % nj, 0)),
            pl.BlockSpec((1, ATTN_WIDTH, tm), lambda s: (front(s) // nj, 0, front(s) % nj)),
            pl.BlockSpec((1, tm, POOL_WIDTH), lambda s: (front(s) // nj, front(s) % nj, 0)),
            pl.BlockSpec((1, POOL_HALO, POOL_WIDTH),
                         lambda s: (front(s) // nj,
                                    jnp.maximum((front(s) % nj) * halo_blocks - 1, 0), 0)),
            const(wpool.shape), const(pscale.shape), const(wout.shape), const(g2.shape),
            const(wg.shape), const(wu.shape), const(wd.shape), const(gf.shape),
        ],
        out_specs=pl.BlockSpec((1, tm, D), lambda s: (back(s) // nj, back(s) % nj, 0)),
        out_shape=jax.ShapeDtypeStruct((B, S, D), jnp.float32),
        scratch_shapes=[pltpu.VMEM((POOL_PAD + POOL_HALO + tm, POOL_WIDTH), jnp.float32),
                        pltpu.VMEM((2, POOL_PAD + POOL_HALO + tm, POOL_GROUP), jnp.float32),
                        pltpu.VMEM((2, tm, D), jnp.float32),
                        pltpu.VMEM((2, tm, D), jnp.bfloat16)],
        compiler_params=pltpu.CompilerParams(
            dimension_semantics=("arbitrary",), vmem_limit_bytes=VMEM_LIMIT),
        name="mix_ffn",
    )(x, attn_t, u, u, wpool, pscale, wout, g2, wg, wu, wd, gf)


def kernel(x, norm1_g, w_in, b_forget, w_pool, pool_scale, w_out, norm2_g, w_gate, w_up, w_down, final_g):
    depth = w_in.shape[0]
    bf16 = jnp.bfloat16
    a0 = ATTN_WIDTH
    n_bias = N_BIAS_TERMS * N_HEADS
    lane = np.arange(LANES)
    mod3 = jnp.asarray(np.where(lane < n_bias, lane % N_BIAS_TERMS, N_BIAS_TERMS)[None, :], jnp.int32)
    for layer in range(depth):
        w = w_in[layer]
        wqk = w[:, :2 * a0].astype(bf16)
        wvt = w[:, 2 * a0:3 * a0].T.astype(bf16)
        wf = jnp.repeat(w[:, 3 * a0:3 * a0 + N_HEADS], N_BIAS_TERMS, axis=1)
        wuf = jnp.concatenate([w[:, 3 * a0 + N_HEADS:], wf,
                               jnp.zeros((D_MODEL, LANES - n_bias), w.dtype)], axis=1).astype(bf16)
        bf = jnp.concatenate([jnp.repeat(b_forget[layer], N_BIAS_TERMS),
                              jnp.zeros((LANES - n_bias,), b_forget.dtype)]).astype(jnp.float32)[None, :]
        qp, kp, vtp, u = _in_proj(x, norm1_g[layer][None, :], wqk, wvt, wuf, bf, mod3)
        attn_t = _fox_attn(qp, kp, vtp)
        x = _mix_ffn(x, attn_t, u, w_pool[layer].astype(bf16), pool_scale[layer][None, :],
                     w_out[layer].astype(bf16), norm2_g[layer][None, :], w_gate[layer].astype(bf16),
                     w_up[layer].astype(bf16), w_down[layer].astype(bf16), final_g[None, :],
                     final_norm=layer == depth - 1)
    return x
```

```python
import functools
import math

import jax
import jax.numpy as jnp
import numpy as np
from jax import lax
from jax.experimental import pallas as pl
from jax.experimental.pallas import tpu as pltpu

D_MODEL = 1024
ATTN_WIDTH = 512
N_HEADS = 8
HEAD_DIM = 64
POOL_WIDTH = 512
POOL_WINDOWS = (2, 4, 8, 16)
POOL_GROUP = 128
D_FF = 2816
EPS = 1e-6

LANES = 128
HEAD_SLOT = 128
N_BIAS_TERMS = 3
VT_ROWS = HEAD_DIM + 16
POOL_HALO = 16
POOL_PAD = 8
LOG2E = math.log2(math.e)
NEG_BIG = -1e30

TOKEN_TILE = 512
ATTN_BLOCK = 512
ATTN_UNIT = 256
ATTN_HEADS_PER_STEP = 4
ATTN_ISSUE_AHEAD = 2
FF_CHUNK = 1408
VMEM_LIMIT = 56 * 1024 * 1024

_NT = (((1,), (1,)), ((), ()))
_TN = (((0,), (0,)), ((), ()))


def _rms(x, g):
    ms = jnp.mean(x * x, axis=-1, keepdims=True)
    return x * lax.rsqrt(ms + EPS) * g


def _split_bf16(x):
    hi = x.astype(jnp.bfloat16)
    r1 = x - hi.astype(jnp.float32)
    mid = r1.astype(jnp.bfloat16)
    lo = (r1 - mid.astype(jnp.float32)).astype(jnp.bfloat16)
    return hi, mid, lo


def _in_proj_kernel(x_ref, g_ref, wqk_ref, wvt_ref, wuf_ref, bf_ref, mod3_ref,
                    qp_ref, kp_ref, vt_ref, u_ref, carry_ref, scan_ref):
    tm = x_ref.shape[1]
    pad = scan_ref.shape[1] - tm

    @pl.when(pl.program_id(1) == 0)
    def _():
        carry_ref[...] = jnp.zeros_like(carry_ref)

    h = _rms(x_ref[0], g_ref[...]).astype(jnp.bfloat16)
    uf = jnp.dot(h, wuf_ref[...], preferred_element_type=jnp.float32)
    u_ref[0] = uf[:, :POOL_WIDTH]
    f = uf[:, POOL_WIDTH:] + bf_ref[...]
    vt = lax.dot_general(wvt_ref[...], h, _NT, preferred_element_type=jnp.float32)
    qk = jnp.dot(h, wqk_ref[...], preferred_element_type=jnp.float32)

    lf = jnp.minimum(f, 0.0) - jnp.log(1.0 + jnp.exp(-jnp.abs(f)))
    scan_ref[:, :pad, :] = jnp.zeros((scan_ref.shape[0], pad, LANES), jnp.float32)
    shift, level = 1, 0
    while shift < tm:
        scan_ref[level, pad:, :] = lf
        lf = lf + scan_ref[level, pad - shift:pad - shift + tm, :]
        shift, level = 2 * shift, 1 - level
    c = lf + carry_ref[...]
    carry_ref[...] = c[tm - 1:tm, :]

    hi, mid, lo = _split_bf16(c * (-LOG2E))
    m3 = mod3_ref[...]
    e3 = jnp.where(m3 == 0, hi.astype(jnp.float32),
                   jnp.where(m3 == 1, mid.astype(jnp.float32), lo.astype(jnp.float32)))

    lane = lax.broadcasted_iota(jnp.int32, (tm, LANES), 1)
    low_half = lane < HEAD_DIM
    ones_hi = jnp.where((lane >= HEAD_DIM) & (lane < HEAD_DIM + N_BIAS_TERMS), 1.0, 0.0)
    ones_lo = jnp.where(lane < N_BIAS_TERMS, 1.0, 0.0)
    q_scale = HEAD_DIM ** -0.5 * LOG2E
    for hd in range(N_HEADS):
        pair = hd // 2
        qpair = qk[:, pair * LANES:(pair + 1) * LANES] * q_scale
        kpair = qk[:, ATTN_WIDTH + pair * LANES:ATTN_WIDTH + (pair + 1) * LANES]
        if hd % 2 == 0:
            bias = pltpu.roll(e3, HEAD_DIM - N_BIAS_TERMS * hd, 1)
            qh = jnp.where(low_half, qpair, ones_hi)
            kh = jnp.where(low_half, kpair, bias)
        else:
            bias = pltpu.roll(e3, LANES - N_BIAS_TERMS * hd, 1)
            qh = jnp.where(low_half, ones_lo, qpair)
            kh = jnp.where(low_half, bias, kpair)
        qp_ref[0, :, hd * HEAD_SLOT:(hd + 1) * HEAD_SLOT] = qh.astype(jnp.bfloat16)
        kp_ref[0, :, hd * HEAD_SLOT:(hd + 1) * HEAD_SLOT] = kh.astype(jnp.bfloat16)
        vt_ref[0, hd * VT_ROWS:hd * VT_ROWS + HEAD_DIM, :] = (
            vt[hd * HEAD_DIM:(hd + 1) * HEAD_DIM, :].astype(jnp.bfloat16))
        vt_ref[0, hd * VT_ROWS + HEAD_DIM:(hd + 1) * VT_ROWS, :] = jnp.ones(
            (VT_ROWS - HEAD_DIM, tm), jnp.bfloat16)


def _in_proj(x, g, wqk, wvt, wuf, bf, mod3):
    B, S, D = x.shape
    tm = TOKEN_TILE
    const = lambda shape: pl.BlockSpec(shape, lambda b, j: (0,) * len(shape))
    return pl.pallas_call(
        _in_proj_kernel,
        grid=(B, S // tm),
        in_specs=[
            pl.BlockSpec((1, tm, D), lambda b, j: (b, j, 0)),
            const(g.shape), const(wqk.shape), const(wvt.shape), const(wuf.shape),
            const(bf.shape), const(mod3.shape),
        ],
        out_specs=[
            pl.BlockSpec((1, tm, N_HEADS * HEAD_SLOT), lambda b, j: (b, j, 0)),
            pl.BlockSpec((1, tm, N_HEADS * HEAD_SLOT), lambda b, j: (b, j, 0)),
            pl.BlockSpec((1, N_HEADS * VT_ROWS, tm), lambda b, j: (b, 0, j)),
            pl.BlockSpec((1, tm, POOL_WIDTH), lambda b, j: (b, j, 0)),
        ],
        out_shape=[
            jax.ShapeDtypeStruct((B, S, N_HEADS * HEAD_SLOT), jnp.bfloat16),
            jax.ShapeDtypeStruct((B, S, N_HEADS * HEAD_SLOT), jnp.bfloat16),
            jax.ShapeDtypeStruct((B, N_HEADS * VT_ROWS, S), jnp.bfloat16),
            jax.ShapeDtypeStruct((B, S, POOL_WIDTH), jnp.float32),
        ],
        scratch_shapes=[pltpu.VMEM((1, LANES), jnp.float32),
                        pltpu.VMEM((2, tm // 2 + tm, LANES), jnp.float32)],
        compiler_params=pltpu.CompilerParams(
            dimension_semantics=("arbitrary", "arbitrary"), vmem_limit_bytes=VMEM_LIMIT),
        name="in_proj",
    )(x, g, wqk, wvt, wuf, bf, mod3)


def _fox_attn_kernel(q_ref, k_ref, vt_ref, o_ref, m_ref, acc_ref, pend_ref):
    S = q_ref.shape[1]
    blk, sub = ATTN_BLOCK, ATTN_UNIT
    heads = q_ref.shape[2] // HEAD_SLOT
    groups = blk // sub
    n_q = S // blk
    tiles = [(ks, hd, qg) for ks in range(groups) for hd in range(heads) for qg in range(groups)]

    def scores(hd, qg, q0, k0):
        lanes = slice(hd * HEAD_SLOT, (hd + 1) * HEAD_SLOT)
        qt = q_ref[0, pl.ds(q0 + qg * sub, sub), lanes]
        kt = k_ref[0, pl.ds(k0, sub), lanes]
        return lax.dot_general(kt, qt, _NT, preferred_element_type=jnp.float32)

    def update(hd, k0, st, m_old, acc_old):
        m_new = jnp.maximum(m_old, jnp.max(st, axis=0, keepdims=True))
        p = jnp.exp2(st - m_new).astype(jnp.bfloat16)
        alpha = jnp.exp2(m_old - m_new)
        vt = vt_ref[0, hd * VT_ROWS:(hd + 1) * VT_ROWS, pl.ds(k0, sub)]
        return m_new, alpha * acc_old + jnp.dot(vt, p, preferred_element_type=jnp.float32)

    def step(slot, k0, nxt_q0, nxt_k0, diag):
        for static_slot in range(2):
            @pl.when(slot == static_slot)
            def _():
                static_step(static_slot, k0, nxt_q0, nxt_k0, diag)

    def static_step(slot, k0, nxt_q0, nxt_k0, diag):
        state = {(hd, qg): (m_ref[hd, :, qg * sub:(qg + 1) * sub],
                            acc_ref[hd, :, qg * sub:(qg + 1) * sub])
                 for hd in range(heads) for qg in range(groups)}

        def issue(n):
            ks, hd, qg = tiles[n]
            pend_ref[1 - slot, n] = scores(hd, qg, nxt_q0, nxt_k0 + ks * sub)

        for n in range(min(ATTN_ISSUE_AHEAD, len(tiles))):
            issue(n)
        for n, (ks, hd, qg) in enumerate(tiles):
            if n + ATTN_ISSUE_AHEAD < len(tiles):
                issue(n + ATTN_ISSUE_AHEAD)
            if diag and qg < ks:
                continue
            st = pend_ref[slot, n]
            if diag and qg == ks:
                key = lax.broadcasted_iota(jnp.int32, (sub, sub), 0)
                qry = lax.broadcasted_iota(jnp.int32, (sub, sub), 1)
                st = jnp.where(key <= qry, st, NEG_BIG)
            state[hd, qg] = update(hd, k0 + ks * sub, st, *state[hd, qg])
        for (hd, qg), (m, acc) in state.items():
            m_ref[hd, :, qg * sub:(qg + 1) * sub] = m
            acc_ref[hd, :, qg * sub:(qg + 1) * sub] = acc

    def q_block(j, slot):
        q0 = pl.multiple_of(j * blk, blk)
        m_ref[...] = jnp.full(m_ref.shape, NEG_BIG, jnp.float32)
        acc_ref[...] = jnp.zeros(acc_ref.shape, jnp.float32)

        def k_block(i, slot):
            k0 = pl.multiple_of(i * blk, blk)
            step(slot, k0, q0, k0 + blk, diag=False)
            return 1 - slot

        slot = lax.fori_loop(0, j, k_block, slot)
        nxt_q0 = pl.multiple_of(jnp.minimum(j + 1, n_q - 1) * blk, blk)
        step(slot, q0, nxt_q0, 0, diag=True)

        for hd in range(heads):
            acc = acc_ref[hd]
            out = acc[:HEAD_DIM, :] / acc[HEAD_DIM:HEAD_DIM + 1, :]
            o_ref[0, hd * HEAD_DIM:(hd + 1) * HEAD_DIM, pl.ds(q0, blk)] = out.astype(o_ref.dtype)
        return 1 - slot

    for n, (ks, hd, qg) in enumerate(tiles):
        pend_ref[0, n] = scores(hd, qg, 0, ks * sub)
    lax.fori_loop(0, n_q, q_block, 0)


def _fox_attn(qp, kp, vtp):
    B, S, _ = qp.shape
    hp = ATTN_HEADS_PER_STEP
    return pl.pallas_call(
        _fox_attn_kernel,
        grid=(B, N_HEADS // hp),
        in_specs=[
            pl.BlockSpec((1, S, hp * HEAD_SLOT), lambda b, h: (b, 0, h)),
            pl.BlockSpec((1, S, hp * HEAD_SLOT), lambda b, h: (b, 0, h)),
            pl.BlockSpec((1, hp * VT_ROWS, S), lambda b, h: (b, h, 0)),
        ],
        out_specs=pl.BlockSpec((1, hp * HEAD_DIM, S), lambda b, h: (b, h, 0)),
        out_shape=jax.ShapeDtypeStruct((B, ATTN_WIDTH, S), jnp.bfloat16),
        scratch_shapes=[pltpu.VMEM((hp, 1, ATTN_BLOCK), jnp.float32),
                        pltpu.VMEM((hp, VT_ROWS, ATTN_BLOCK), jnp.float32),
                        pltpu.VMEM((2, hp * (ATTN_BLOCK // ATTN_UNIT) ** 2, ATTN_UNIT, ATTN_UNIT),
                                   jnp.float32)],
        compiler_params=pltpu.CompilerParams(
            dimension_semantics=("arbitrary", "arbitrary"), vmem_limit_bytes=VMEM_LIMIT),
        name="fox_attn",
    )(qp, kp, vtp)


def _mix_ffn_kernel(x_ref, at_ref, u_ref, uprev_ref, wpool_ref, pscale_ref, wout_ref, g2_ref,
                    wg_ref, wu_ref, wd_ref, gf_ref, y_ref, ext_ref, stage_ref, *, final_norm):
    tm = x_ref.shape[1]
    j = pl.program_id(1)

    n_groups = len(POOL_WINDOWS)
    out_chunk = D_MODEL // n_groups

    pad, top = POOL_PAD, POOL_PAD + POOL_HALO
    u = u_ref[0]
    ext_ref[:pad, :] = jnp.zeros((pad, POOL_WIDTH), jnp.float32)
    ext_ref[pad:top, :] = jnp.where(j > 0, uprev_ref[0], 0.0)
    ext_ref[top:, :] = u
    stage_ref[:, :pad, :] = jnp.zeros((stage_ref.shape[0], pad, POOL_GROUP), jnp.float32)

    first_rows = lax.broadcasted_iota(jnp.int32, (POOL_HALO, POOL_GROUP), 0) + 1
    mixed, x1_cols = [], []
    for gi, w in enumerate(POOL_WINDOWS):
        ocols = slice(gi * out_chunk, (gi + 1) * out_chunk)
        x1_cols.append(x_ref[0, :, ocols]
                       + lax.dot_general(at_ref[0], wout_ref[:ATTN_WIDTH, ocols], _TN,
                                         preferred_element_type=jnp.float32))
        cols = slice(gi * POOL_GROUP, (gi + 1) * POOL_GROUP)
        end = ext_ref.shape[0]
        wsum = ext_ref[pad:end, cols] + ext_ref[pad - 1:end - 1, cols]
        span = 2
        while span < w:
            level = stage_ref.at[(span.bit_length() - 2) % stage_ref.shape[0]]
            level[pad:end, :] = wsum
            wsum = wsum + level[pad - span:end - span, :]
            span *= 2
        wsum = wsum[POOL_HALO:, :]
        mean = wsum * (1.0 / w)
        exact_head = wsum[:POOL_HALO, :] / jnp.minimum(first_rows, w).astype(jnp.float32)
        head = jnp.where(j == 0, exact_head, mean[:POOL_HALO, :])
        mean = jnp.concatenate([head, mean[POOL_HALO:, :]], axis=0)
        pooled = (mean - u[:, cols]).astype(jnp.bfloat16)
        mg = jnp.dot(pooled, wpool_ref[gi], preferred_element_type=jnp.float32)
        mixed.append((mg * pscale_ref[:, cols]).astype(jnp.bfloat16))
    pool = jnp.concatenate(mixed, axis=1)
    x1 = jnp.concatenate(x1_cols, axis=1) + jnp.dot(pool, wout_ref[ATTN_WIDTH:, :],
                                                    preferred_element_type=jnp.float32)

    h2 = _rms(x1, g2_ref[...]).astype(jnp.bfloat16)
    ffn = jnp.zeros((tm, D_MODEL), jnp.float32)
    for c0 in range(0, D_FF, FF_CHUNK):
        gate = jnp.dot(h2, wg_ref[:, c0:c0 + FF_CHUNK], preferred_element_type=jnp.float32)
        up = jnp.dot(h2, wu_ref[:, c0:c0 + FF_CHUNK], preferred_element_type=jnp.float32)
        act = (gate / (1.0 + jnp.exp(-gate)) * up).astype(jnp.bfloat16)
        ffn = ffn + jnp.dot(act, wd_ref[c0:c0 + FF_CHUNK, :], preferred_element_type=jnp.float32)

    x2 = x1 + ffn
    y_ref[0] = _rms(x2, gf_ref[...]) if final_norm else x2


def _mix_ffn(x, attn_t, u, wpool, pscale, wout, g2, wg, wu, wd, gf, final_norm):
    B, S, D = x.shape
    tm = TOKEN_TILE
    halo_blocks = tm // POOL_HALO
    const = lambda shape: pl.BlockSpec(shape, lambda b, j: (0,) * len(shape),
                                       pipeline_mode=pl.Buffered(1))
    return pl.pallas_call(
        functools.partial(_mix_ffn_kernel, final_norm=final_norm),
        grid=(B, S // tm),
        in_specs=[
            pl.BlockSpec((1, tm, D), lambda b, j: (b, j, 0)),
            pl.BlockSpec((1, ATTN_WIDTH, tm), lambda b, j: (b, 0, j)),
            pl.BlockSpec((1, tm, POOL_WIDTH), lambda b, j: (b, j, 0)),
            pl.BlockSpec((1, POOL_HALO, POOL_WIDTH),
                         lambda b, j: (b, jnp.maximum(j * halo_blocks - 1, 0), 0)),
            const(wpool.shape), const(pscale.shape), const(wout.shape), const(g2.shape),
            const(wg.shape), const(wu.shape), const(wd.shape), const(gf.shape),
        ],
        out_specs=pl.BlockSpec((1, tm, D), lambda b, j: (b, j, 0)),
        out_shape=jax.ShapeDtypeStruct((B, S, D), jnp.float32),
        scratch_shapes=[pltpu.VMEM((POOL_PAD + POOL_HALO + tm, POOL_WIDTH), jnp.float32),
                        pltpu.VMEM((2, POOL_PAD + POOL_HALO + tm, POOL_GROUP), jnp.float32)],
        compiler_params=pltpu.CompilerParams(
            dimension_semantics=("arbitrary", "arbitrary"), vmem_limit_bytes=VMEM_LIMIT),
        name="mix_ffn",
    )(x, attn_t, u, u, wpool, pscale, wout, g2, wg, wu, wd, gf)


def kernel(x, norm1_g, w_in, b_forget, w_pool, pool_scale, w_out, norm2_g, w_gate, w_up, w_down, final_g):
    depth = w_in.shape[0]
    bf16 = jnp.bfloat16
    a0 = ATTN_WIDTH
    n_bias = N_BIAS_TERMS * N_HEADS
    lane = np.arange(LANES)
    mod3 = jnp.asarray(np.where(lane < n_bias, lane % N_BIAS_TERMS, N_BIAS_TERMS)[None, :], jnp.int32)
    for layer in range(depth):
        w = w_in[layer]
        wqk = w[:, :2 * a0].astype(bf16)
        wvt = w[:, 2 * a0:3 * a0].T.astype(bf16)
        wf = jnp.repeat(w[:, 3 * a0:3 * a0 + N_HEADS], N_BIAS_TERMS, axis=1)
        wuf = jnp.concatenate([w[:, 3 * a0 + N_HEADS:], wf,
                               jnp.zeros((D_MODEL, LANES - n_bias), w.dtype)], axis=1).astype(bf16)
        bf = jnp.concatenate([jnp.repeat(b_forget[layer], N_BIAS_TERMS),
                              jnp.zeros((LANES - n_bias,), b_forget.dtype)]).astype(jnp.float32)[None, :]
        qp, kp, vtp, u = _in_proj(x, norm1_g[layer][None, :], wqk, wvt, wuf, bf, mod3)
        attn_t = _fox_attn(qp, kp, vtp)
        x = _mix_ffn(x, attn_t, u, w_pool[layer].astype(bf16), pool_scale[layer][None, :],
                     w_out[layer].astype(bf16), norm2_g[layer][None, :], w_gate[layer].astype(bf16),
                     w_up[layer].astype(bf16), w_down[layer].astype(bf16), final_g[None, :],
                     final_norm=layer == depth - 1)
    return x
```

```python
import functools
import math

import jax
import jax.numpy as jnp
import numpy as np
from jax import lax
from jax.experimental import pallas as pl
from jax.experimental.pallas import tpu as pltpu

D_MODEL = 1024
ATTN_WIDTH = 512
N_HEADS = 8
HEAD_DIM = 64
POOL_WIDTH = 512
POOL_WINDOWS = (2, 4, 8, 16)
POOL_GROUP = 128
D_FF = 2816
EPS = 1e-6

LANES = 128
HEAD_SLOT = 128
N_BIAS_TERMS = 3
VT_ROWS = HEAD_DIM + 16
POOL_HALO = 16
POOL_PAD = 8
LOG2E = math.log2(math.e)
NEG_BIG = -1e30

TOKEN_TILE = 1024
IN_PROJ_TILE = 1024
ATTN_BLOCK = 512
ATTN_UNIT = 256
ATTN_HEADS_PER_STEP = 4
ATTN_ISSUE_AHEAD = 2
FF_CHUNK = 256
VMEM_LIMIT = 56 * 1024 * 1024

_NT = (((1,), (1,)), ((), ()))
_TN = (((0,), (0,)), ((), ()))


def _rms(x, g):
    ms = jnp.mean(x * x, axis=-1, keepdims=True)
    return x * lax.rsqrt(ms + EPS) * g


def _split_bf16(x):
    hi = x.astype(jnp.bfloat16)
    r1 = x - hi.astype(jnp.float32)
    mid = r1.astype(jnp.bfloat16)
    lo = (r1 - mid.astype(jnp.float32)).astype(jnp.bfloat16)
    return hi, mid, lo


def _in_proj_kernel(x_ref, g_ref, wqk_ref, wvt_ref, wuf_ref, bf_ref, mod3_ref,
                    qp_ref, kp_ref, vt_ref, u_ref, carry_ref, scan_ref):
    tm = x_ref.shape[1]
    pad = scan_ref.shape[1] - tm

    @pl.when(pl.program_id(1) == 0)
    def _():
        carry_ref[...] = jnp.zeros_like(carry_ref)

    h = _rms(x_ref[0], g_ref[...]).astype(jnp.bfloat16)
    uf = jnp.dot(h, wuf_ref[...], preferred_element_type=jnp.float32)
    u_ref[0] = uf[:, :POOL_WIDTH]
    f = uf[:, POOL_WIDTH:] + bf_ref[...]
    vt = lax.dot_general(wvt_ref[...], h, _NT, preferred_element_type=jnp.float32)
    qk = jnp.dot(h, wqk_ref[...], preferred_element_type=jnp.float32)

    lf = jnp.minimum(f, 0.0) - jnp.log(1.0 + jnp.exp(-jnp.abs(f)))
    scan_ref[:, :pad, :] = jnp.zeros((scan_ref.shape[0], pad, LANES), jnp.float32)
    shift, level = 1, 0
    while shift < tm:
        scan_ref[level, pad:, :] = lf
        lf = lf + scan_ref[level, pad - shift:pad - shift + tm, :]
        shift, level = 2 * shift, 1 - level
    c = lf + carry_ref[...]
    carry_ref[...] = c[tm - 1:tm, :]

    hi, mid, lo = _split_bf16(c * (-LOG2E))
    m3 = mod3_ref[...]
    e3 = jnp.where(m3 == 0, hi.astype(jnp.float32),
                   jnp.where(m3 == 1, mid.astype(jnp.float32), lo.astype(jnp.float32)))

    lane = lax.broadcasted_iota(jnp.int32, (tm, LANES), 1)
    low_half = lane < HEAD_DIM
    ones_hi = jnp.where((lane >= HEAD_DIM) & (lane < HEAD_DIM + N_BIAS_TERMS), 1.0, 0.0)
    ones_lo = jnp.where(lane < N_BIAS_TERMS, 1.0, 0.0)
    q_scale = HEAD_DIM ** -0.5 * LOG2E
    for hd in range(N_HEADS):
        pair = hd // 2
        qpair = qk[:, pair * LANES:(pair + 1) * LANES] * q_scale
        kpair = qk[:, ATTN_WIDTH + pair * LANES:ATTN_WIDTH + (pair + 1) * LANES]
        if hd % 2 == 0:
            bias = pltpu.roll(e3, HEAD_DIM - N_BIAS_TERMS * hd, 1)
            qh = jnp.where(low_half, qpair, ones_hi)
            kh = jnp.where(low_half, kpair, bias)
        else:
            bias = pltpu.roll(e3, LANES - N_BIAS_TERMS * hd, 1)
            qh = jnp.where(low_half, ones_lo, qpair)
            kh = jnp.where(low_half, bias, kpair)
        qp_ref[0, :, hd * HEAD_SLOT:(hd + 1) * HEAD_SLOT] = qh.astype(jnp.bfloat16)
        kp_ref[0, :, hd * HEAD_SLOT:(hd + 1) * HEAD_SLOT] = kh.astype(jnp.bfloat16)
        vt_ref[0, hd * VT_ROWS:hd * VT_ROWS + HEAD_DIM, :] = (
            vt[hd * HEAD_DIM:(hd + 1) * HEAD_DIM, :].astype(jnp.bfloat16))
        vt_ref[0, hd * VT_ROWS + HEAD_DIM:(hd + 1) * VT_ROWS, :] = jnp.ones(
            (VT_ROWS - HEAD_DIM, tm), jnp.bfloat16)


def _in_proj(x, g, wqk, wvt, wuf, bf, mod3):
    B, S, D = x.shape
    tm = IN_PROJ_TILE
    const = lambda shape: pl.BlockSpec(shape, lambda b, j: (0,) * len(shape))
    return pl.pallas_call(
        _in_proj_kernel,
        grid=(B, S // tm),
        in_specs=[
            pl.BlockSpec((1, tm, D), lambda b, j: (b, j, 0)),
            const(g.shape), const(wqk.shape), const(wvt.shape), const(wuf.shape),
            const(bf.shape), const(mod3.shape),
        ],
        out_specs=[
            pl.BlockSpec((1, tm, N_HEADS * HEAD_SLOT), lambda b, j: (b, j, 0)),
            pl.BlockSpec((1, tm, N_HEADS * HEAD_SLOT), lambda b, j: (b, j, 0)),
            pl.BlockSpec((1, N_HEADS * VT_ROWS, tm), lambda b, j: (b, 0, j)),
            pl.BlockSpec((1, tm, POOL_WIDTH), lambda b, j: (b, j, 0)),
        ],
        out_shape=[
            jax.ShapeDtypeStruct((B, S, N_HEADS * HEAD_SLOT), jnp.bfloat16),
            jax.ShapeDtypeStruct((B, S, N_HEADS * HEAD_SLOT), jnp.bfloat16),
            jax.ShapeDtypeStruct((B, N_HEADS * VT_ROWS, S), jnp.bfloat16),
            jax.ShapeDtypeStruct((B, S, POOL_WIDTH), jnp.float32),
        ],
        scratch_shapes=[pltpu.VMEM((1, LANES), jnp.float32),
                        pltpu.VMEM((2, tm // 2 + tm, LANES), jnp.float32)],
        compiler_params=pltpu.CompilerParams(
            dimension_semantics=("arbitrary", "arbitrary"), vmem_limit_bytes=VMEM_LIMIT),
        name="in_proj",
    )(x, g, wqk, wvt, wuf, bf, mod3)


def _fox_attn_kernel(q_ref, k_ref, vt_ref, o_ref, m_ref, acc_ref, pend_ref):
    S = q_ref.shape[1]
    blk, sub = ATTN_BLOCK, ATTN_UNIT
    heads = q_ref.shape[2] // HEAD_SLOT
    groups = blk // sub
    n_q = S // blk
    tiles = [(ks, hd, qg) for ks in range(groups) for hd in range(heads) for qg in range(groups)]

    def scores(hd, qg, q0, k0):
        lanes = slice(hd * HEAD_SLOT, (hd + 1) * HEAD_SLOT)
        qt = q_ref[0, pl.ds(q0 + qg * sub, sub), lanes]
        kt = k_ref[0, pl.ds(k0, sub), lanes]
        return lax.dot_general(kt, qt, _NT, preferred_element_type=jnp.float32)

    def update(hd, k0, st, m_old, acc_old):
        m_new = jnp.maximum(m_old, jnp.max(st, axis=0, keepdims=True))
        p = jnp.exp2(st - m_new).astype(jnp.bfloat16)
        alpha = jnp.exp2(m_old - m_new)
        vt = vt_ref[0, hd * VT_ROWS:(hd + 1) * VT_ROWS, pl.ds(k0, sub)]
        return m_new, alpha * acc_old + jnp.dot(vt, p, preferred_element_type=jnp.float32)

    def step(slot, k0, nxt_q0, nxt_k0, diag):
        for static_slot in range(2):
            @pl.when(slot == static_slot)
            def _():
                static_step(static_slot, k0, nxt_q0, nxt_k0, diag)

    def static_step(slot, k0, nxt_q0, nxt_k0, diag):
        state = {(hd, qg): (m_ref[hd, :, qg * sub:(qg + 1) * sub],
                            acc_ref[hd, :, qg * sub:(qg + 1) * sub])
                 for hd in range(heads) for qg in range(groups)}

        def issue(n):
            ks, hd, qg = tiles[n]
            pend_ref[1 - slot, n] = scores(hd, qg, nxt_q0, nxt_k0 + ks * sub)

        for n in range(min(ATTN_ISSUE_AHEAD, len(tiles))):
            issue(n)
        for n, (ks, hd, qg) in enumerate(tiles):
            if n + ATTN_ISSUE_AHEAD < len(tiles):
                issue(n + ATTN_ISSUE_AHEAD)
            if diag and qg < ks:
                continue
            st = pend_ref[slot, n]
            if diag and qg == ks:
                key = lax.broadcasted_iota(jnp.int32, (sub, sub), 0)
                qry = lax.broadcasted_iota(jnp.int32, (sub, sub), 1)
                st = jnp.where(key <= qry, st, NEG_BIG)
            state[hd, qg] = update(hd, k0 + ks * sub, st, *state[hd, qg])
        for (hd, qg), (m, acc) in state.items():
            m_ref[hd, :, qg * sub:(qg + 1) * sub] = m
            acc_ref[hd, :, qg * sub:(qg + 1) * sub] = acc

    def q_block(j, slot):
        q0 = pl.multiple_of(j * blk, blk)
        m_ref[...] = jnp.full(m_ref.shape, NEG_BIG, jnp.float32)
        acc_ref[...] = jnp.zeros(acc_ref.shape, jnp.float32)

        def k_block(i, slot):
            k0 = pl.multiple_of(i * blk, blk)
            step(slot, k0, q0, k0 + blk, diag=False)
            return 1 - slot

        slot = lax.fori_loop(0, j, k_block, slot)
        nxt_q0 = pl.multiple_of(jnp.minimum(j + 1, n_q - 1) * blk, blk)
        step(slot, q0, nxt_q0, 0, diag=True)

        for hd in range(heads):
            acc = acc_ref[hd]
            out = acc[:HEAD_DIM, :] / acc[HEAD_DIM:HEAD_DIM + 1, :]
            o_ref[0, hd * HEAD_DIM:(hd + 1) * HEAD_DIM, pl.ds(q0, blk)] = out.astype(o_ref.dtype)
        return 1 - slot

    for n, (ks, hd, qg) in enumerate(tiles):
        pend_ref[0, n] = scores(hd, qg, 0, ks * sub)
    lax.fori_loop(0, n_q, q_block, 0)


def _fox_attn(qp, kp, vtp):
    B, S, _ = qp.shape
    hp = ATTN_HEADS_PER_STEP
    return pl.pallas_call(
        _fox_attn_kernel,
        grid=(B, N_HEADS // hp),
        in_specs=[
            pl.BlockSpec((1, S, hp * HEAD_SLOT), lambda b, h: (b, 0, h)),
            pl.BlockSpec((1, S, hp * HEAD_SLOT), lambda b, h: (b, 0, h)),
            pl.BlockSpec((1, hp * VT_ROWS, S), lambda b, h: (b, h, 0)),
        ],
        out_specs=pl.BlockSpec((1, hp * HEAD_DIM, S), lambda b, h: (b, h, 0)),
        out_shape=jax.ShapeDtypeStruct((B, ATTN_WIDTH, S), jnp.bfloat16),
        scratch_shapes=[pltpu.VMEM((hp, 1, ATTN_BLOCK), jnp.float32),
                        pltpu.VMEM((hp, VT_ROWS, ATTN_BLOCK), jnp.float32),
                        pltpu.VMEM((2, hp * (ATTN_BLOCK // ATTN_UNIT) ** 2, ATTN_UNIT, ATTN_UNIT),
                                   jnp.float32)],
        compiler_params=pltpu.CompilerParams(
            dimension_semantics=("arbitrary", "arbitrary"), vmem_limit_bytes=VMEM_LIMIT),
        name="fox_attn",
    )(qp, kp, vtp)


def _mix_ffn_kernel(x_ref, at_ref, u_ref, uprev_ref, wpool_ref, pscale_ref, wout_ref, g2_ref,
                    wg_ref, wu_ref, wd_ref, gf_ref, y_ref, ext_ref, stage_ref, *, final_norm):
    tm = x_ref.shape[1]
    j = pl.program_id(1)

    n_groups = len(POOL_WINDOWS)
    out_chunk = D_MODEL // n_groups

    pad, top = POOL_PAD, POOL_PAD + POOL_HALO
    u = u_ref[0]
    ext_ref[:pad, :] = jnp.zeros((pad, POOL_WIDTH), jnp.float32)
    ext_ref[pad:top, :] = jnp.where(j > 0, uprev_ref[0], 0.0)
    ext_ref[top:, :] = u
    stage_ref[:, :pad, :] = jnp.zeros((stage_ref.shape[0], pad, POOL_GROUP), jnp.float32)

    first_rows = lax.broadcasted_iota(jnp.int32, (POOL_HALO, POOL_GROUP), 0) + 1
    mixed, x1_cols = [], []
    for gi, w in enumerate(POOL_WINDOWS):
        ocols = slice(gi * out_chunk, (gi + 1) * out_chunk)
        x1_cols.append(x_ref[0, :, ocols]
                       + lax.dot_general(at_ref[0], wout_ref[:ATTN_WIDTH, ocols], _TN,
                                         preferred_element_type=jnp.float32))
        cols = slice(gi * POOL_GROUP, (gi + 1) * POOL_GROUP)
        end = ext_ref.shape[0]
        wsum = ext_ref[pad:end, cols] + ext_ref[pad - 1:end - 1, cols]
        span = 2
        while span < w:
            level = stage_ref.at[(span.bit_length() - 2) % stage_ref.shape[0]]
            level[pad:end, :] = wsum
            wsum = wsum + level[pad - span:end - span, :]
            span *= 2
        wsum = wsum[POOL_HALO:, :]
        mean = wsum * (1.0 / w)
        exact_head = wsum[:POOL_HALO, :] / jnp.minimum(first_rows, w).astype(jnp.float32)
        head = jnp.where(j == 0, exact_head, mean[:POOL_HALO, :])
        mean = jnp.concatenate([head, mean[POOL_HALO:, :]], axis=0)
        pooled = (mean - u[:, cols]).astype(jnp.bfloat16)
        mg = jnp.dot(pooled, wpool_ref[gi], preferred_element_type=jnp.float32)
        mixed.append((mg * pscale_ref[:, cols]).astype(jnp.bfloat16))
    pool = jnp.concatenate(mixed, axis=1)
    x1 = jnp.concatenate(x1_cols, axis=1) + jnp.dot(pool, wout_ref[ATTN_WIDTH:, :],
                                                    preferred_element_type=jnp.float32)

    h2 = _rms(x1, g2_ref[...]).astype(jnp.bfloat16)
    ffn = jnp.zeros((tm, D_MODEL), jnp.float32)
    for c0 in range(0, D_FF, FF_CHUNK):
        gate = jnp.dot(h2, wg_ref[:, c0:c0 + FF_CHUNK], preferred_element_type=jnp.float32)
        up = jnp.dot(h2, wu_ref[:, c0:c0 + FF_CHUNK], preferred_element_type=jnp.float32)
        act = (gate / (1.0 + jnp.exp(-gate)) * up).astype(jnp.bfloat16)
        ffn = ffn + jnp.dot(act, wd_ref[c0:c0 + FF_CHUNK, :], preferred_element_type=jnp.float32)

    x2 = x1 + ffn
    y_ref[0] = _rms(x2, gf_ref[...]) if final_norm else x2


def _mix_ffn(x, attn_t, u, wpool, pscale, wout, g2, wg, wu, wd, gf, final_norm):
    B, S, D = x.shape
    tm = TOKEN_TILE
    halo_blocks = tm // POOL_HALO
    const = lambda shape: pl.BlockSpec(shape, lambda b, j: (0,) * len(shape),
                                       pipeline_mode=pl.Buffered(1))
    return pl.pallas_call(
        functools.partial(_mix_ffn_kernel, final_norm=final_norm),
        grid=(B, S // tm),
        in_specs=[
            pl.BlockSpec((1, tm, D), lambda b, j: (b, j, 0)),
            pl.BlockSpec((1, ATTN_WIDTH, tm), lambda b, j: (b, 0, j)),
            pl.BlockSpec((1, tm, POOL_WIDTH), lambda b, j: (b, j, 0)),
            pl.BlockSpec((1, POOL_HALO, POOL_WIDTH),
                         lambda b, j: (b, jnp.maximum(j * halo_blocks - 1, 0), 0)),
            const(wpool.shape), const(pscale.shape), const(wout.shape), const(g2.shape),
            const(wg.shape), const(wu.shape), const(wd.shape), const(gf.shape),
        ],
        out_specs=pl.BlockSpec((1, tm, D), lambda b, j: (b, j, 0)),
        out_shape=jax.ShapeDtypeStruct((B, S, D), jnp.float32),
        scratch_shapes=[pltpu.VMEM((POOL_PAD + POOL_HALO + tm, POOL_WIDTH), jnp.float32),
                        pltpu.VMEM((2, POOL_PAD + POOL_HALO + tm, POOL_GROUP), jnp.float32)],
        compiler_params=pltpu.CompilerParams(
            dimension_semantics=("arbitrary", "arbitrary"), vmem_limit_bytes=VMEM_LIMIT),
        name="mix_ffn",
    )(x, attn_t, u, u, wpool, pscale, wout, g2, wg, wu, wd, gf)


def kernel(x, norm1_g, w_in, b_forget, w_pool, pool_scale, w_out, norm2_g, w_gate, w_up, w_down, final_g):
    depth = w_in.shape[0]
    bf16 = jnp.bfloat16
    a0 = ATTN_WIDTH
    n_bias = N_BIAS_TERMS * N_HEADS
    lane = np.arange(LANES)
    mod3 = jnp.asarray(np.where(lane < n_bias, lane % N_BIAS_TERMS, N_BIAS_TERMS)[None, :], jnp.int32)
    for layer in range(depth):
        w = w_in[layer]
        wqk = w[:, :2 * a0].astype(bf16)
        wvt = w[:, 2 * a0:3 * a0].T.astype(bf16)
        wf = jnp.repeat(w[:, 3 * a0:3 * a0 + N_HEADS], N_BIAS_TERMS, axis=1)
        wuf = jnp.concatenate([w[:, 3 * a0 + N_HEADS:], wf,
                               jnp.zeros((D_MODEL, LANES - n_bias), w.dtype)], axis=1).astype(bf16)
        bf = jnp.concatenate([jnp.repeat(b_forget[layer], N_BIAS_TERMS),
                              jnp.zeros((LANES - n_bias,), b_forget.dtype)]).astype(jnp.float32)[None, :]
        qp, kp, vtp, u = _in_proj(x, norm1_g[layer][None, :], wqk, wvt, wuf, bf, mod3)
        attn_t = _fox_attn(qp, kp, vtp)
        x = _mix_ffn(x, attn_t, u, w_pool[layer].astype(bf16), pool_scale[layer][None, :],
                     w_out[layer].astype(bf16), norm2_g[layer][None, :], w_gate[layer].astype(bf16),
                     w_up[layer].astype(bf16), w_down[layer].astype(bf16), final_g[None, :],
                     final_norm=layer == depth - 1)
    return x
```

```python
import functools
import math

import jax
import jax.numpy as jnp
import numpy as np
from jax import lax
from jax.experimental import pallas as pl
from jax.experimental.pallas import tpu as pltpu

D_MODEL = 1024
ATTN_WIDTH = 512
N_HEADS = 8
HEAD_DIM = 64
POOL_WIDTH = 512
POOL_WINDOWS = (2, 4, 8, 16)
POOL_GROUP = 128
D_FF = 2816
EPS = 1e-6

LANES = 128
BF16_SUBLANES = 16
HEAD_SLOT = 128
N_BIAS_TERMS = 3
VT_ROWS = HEAD_DIM + 16
POOL_HALO = 16
POOL_PAD = 8
LOG2E = math.log2(math.e)
NEG_BIG = -1e30

TOKEN_TILE = 1024
IN_PROJ_TILE = 1024
ATTN_BLOCK = 512
ATTN_UNIT = 256
ATTN_HEADS_PER_STEP = 4
ATTN_ISSUE_AHEAD = 2
FF_CHUNK = 256
VMEM_LIMIT = 56 * 1024 * 1024

_NT = (((1,), (1,)), ((), ()))
_TN = (((0,), (0,)), ((), ()))


def _rms(x, g):
    ms = jnp.mean(x * x, axis=-1, keepdims=True)
    return x * lax.rsqrt(ms + EPS) * g


def _split_bf16(x):
    hi = x.astype(jnp.bfloat16)
    r1 = x - hi.astype(jnp.float32)
    mid = r1.astype(jnp.bfloat16)
    lo = (r1 - mid.astype(jnp.float32)).astype(jnp.bfloat16)
    return hi, mid, lo


def _in_proj_kernel(x_ref, g_ref, wqk_ref, wvt_ref, wuf_ref, bf_ref, mod3_ref,
                    wg_ref, wu_ref, wd_ref,
                    qp_ref, kp_ref, vt_ref, u_ref, wg16_ref, wu16_ref, wd16_ref,
                    carry_ref, scan_ref):
    tm = x_ref.shape[1]
    pad = scan_ref.shape[1] - tm

    @pl.when(pl.program_id(1) == 0)
    def _():
        carry_ref[...] = jnp.zeros_like(carry_ref)


    h = _rms(x_ref[0], g_ref[...]).astype(jnp.bfloat16)
    uf = jnp.dot(h, wuf_ref[...], preferred_element_type=jnp.float32)
    u_ref[0] = uf[:, :POOL_WIDTH]
    f = uf[:, POOL_WIDTH:] + bf_ref[...]
    vt = lax.dot_general(wvt_ref[...], h, _NT, preferred_element_type=jnp.float32)
    qk = jnp.dot(h, wqk_ref[...], preferred_element_type=jnp.float32)

    wg16_ref[...] = wg_ref[...].astype(wg16_ref.dtype)
    wu16_ref[...] = wu_ref[...].astype(wu16_ref.dtype)
    wd16_ref[...] = wd_ref[...].astype(wd16_ref.dtype)

    lf = jnp.minimum(f, 0.0) - jnp.log(1.0 + jnp.exp(-jnp.abs(f)))
    scan_ref[:, :pad, :] = jnp.zeros((scan_ref.shape[0], pad, LANES), jnp.float32)
    shift, level = 1, 0
    while shift < tm:
        scan_ref[level, pad:, :] = lf
        lf = lf + scan_ref[level, pad - shift:pad - shift + tm, :]
        shift, level = 2 * shift, 1 - level
    c = lf + carry_ref[...]
    carry_ref[...] = c[tm - 1:tm, :]

    hi, mid, lo = _split_bf16(c * (-LOG2E))
    m3 = mod3_ref[...]
    e3 = jnp.where(m3 == 0, hi.astype(jnp.float32),
                   jnp.where(m3 == 1, mid.astype(jnp.float32), lo.astype(jnp.float32)))

    lane = lax.broadcasted_iota(jnp.int32, (tm, LANES), 1)
    low_half = lane < HEAD_DIM
    ones_hi = jnp.where((lane >= HEAD_DIM) & (lane < HEAD_DIM + N_BIAS_TERMS), 1.0, 0.0)
    ones_lo = jnp.where(lane < N_BIAS_TERMS, 1.0, 0.0)
    q_scale = HEAD_DIM ** -0.5 * LOG2E
    for hd in range(N_HEADS):
        pair = hd // 2
        qpair = qk[:, pair * LANES:(pair + 1) * LANES] * q_scale
        kpair = qk[:, ATTN_WIDTH + pair * LANES:ATTN_WIDTH + (pair + 1) * LANES]
        if hd % 2 == 0:
            bias = pltpu.roll(e3, HEAD_DIM - N_BIAS_TERMS * hd, 1)
            qh = jnp.where(low_half, qpair, ones_hi)
            kh = jnp.where(low_half, kpair, bias)
        else:
            bias = pltpu.roll(e3, LANES - N_BIAS_TERMS * hd, 1)
            qh = jnp.where(low_half, ones_lo, qpair)
            kh = jnp.where(low_half, bias, kpair)
        qp_ref[0, :, hd * HEAD_SLOT:(hd + 1) * HEAD_SLOT] = qh.astype(jnp.bfloat16)
        kp_ref[0, :, hd * HEAD_SLOT:(hd + 1) * HEAD_SLOT] = kh.astype(jnp.bfloat16)
        vt_ref[0, hd * VT_ROWS:hd * VT_ROWS + HEAD_DIM, :] = (
            vt[hd * HEAD_DIM:(hd + 1) * HEAD_DIM, :].astype(jnp.bfloat16))
        vt_ref[0, hd * VT_ROWS + HEAD_DIM:(hd + 1) * VT_ROWS, :] = jnp.ones(
            (VT_ROWS - HEAD_DIM, tm), jnp.bfloat16)


def _in_proj(x, g, wqk, wvt, wuf, bf, mod3, ffn_weights):
    B, S, D = x.shape
    tm = IN_PROJ_TILE
    nj = S // tm
    n_steps = B * nj
    const = lambda shape: pl.BlockSpec(shape, lambda b, j: (0,) * len(shape))

    def slab_spec(w):
        rows = w.shape[0]
        n_slabs = max(k for k in range(1, n_steps + 1)
                      if rows % k == 0 and (rows // k) % BF16_SUBLANES == 0)
        return pl.BlockSpec((rows // n_slabs, w.shape[1]),
                            lambda b, j: (jnp.minimum(b * nj + j, n_slabs - 1), 0))

    outs = pl.pallas_call(
        _in_proj_kernel,
        grid=(B, nj),
        in_specs=[
            pl.BlockSpec((1, tm, D), lambda b, j: (b, j, 0)),
            const(g.shape), const(wqk.shape), const(wvt.shape), const(wuf.shape),
            const(bf.shape), const(mod3.shape),
        ] + [slab_spec(w) for w in ffn_weights],
        out_specs=[
            pl.BlockSpec((1, tm, N_HEADS * HEAD_SLOT), lambda b, j: (b, j, 0)),
            pl.BlockSpec((1, tm, N_HEADS * HEAD_SLOT), lambda b, j: (b, j, 0)),
            pl.BlockSpec((1, N_HEADS * VT_ROWS, tm), lambda b, j: (b, 0, j)),
            pl.BlockSpec((1, tm, POOL_WIDTH), lambda b, j: (b, j, 0)),
        ] + [slab_spec(w) for w in ffn_weights],
        out_shape=[
            jax.ShapeDtypeStruct((B, S, N_HEADS * HEAD_SLOT), jnp.bfloat16),
            jax.ShapeDtypeStruct((B, S, N_HEADS * HEAD_SLOT), jnp.bfloat16),
            jax.ShapeDtypeStruct((B, N_HEADS * VT_ROWS, S), jnp.bfloat16),
            jax.ShapeDtypeStruct((B, S, POOL_WIDTH), jnp.float32),
        ] + [jax.ShapeDtypeStruct(w.shape, jnp.bfloat16) for w in ffn_weights],
        scratch_shapes=[pltpu.VMEM((1, LANES), jnp.float32),
                        pltpu.VMEM((2, tm // 2 + tm, LANES), jnp.float32)],
        compiler_params=pltpu.CompilerParams(
            dimension_semantics=("arbitrary", "arbitrary"), vmem_limit_bytes=VMEM_LIMIT),
        name="in_proj",
    )(x, g, wqk, wvt, wuf, bf, mod3, *ffn_weights)
    return outs[:4], outs[4:]


def _fox_attn_kernel(q_ref, k_ref, vt_ref, o_ref, m_ref, acc_ref, pend_ref):
    S = q_ref.shape[1]
    blk, sub = ATTN_BLOCK, ATTN_UNIT
    heads = q_ref.shape[2] // HEAD_SLOT
    groups = blk // sub
    n_q = S // blk
    tiles = [(ks, hd, qg) for ks in range(groups) for hd in range(heads) for qg in range(groups)]

    def scores(hd, qg, q0, k0):
        lanes = slice(hd * HEAD_SLOT, (hd + 1) * HEAD_SLOT)
        qt = q_ref[0, pl.ds(q0 + qg * sub, sub), lanes]
        kt = k_ref[0, pl.ds(k0, sub), lanes]
        return lax.dot_general(kt, qt, _NT, preferred_element_type=jnp.float32)

    def update(hd, k0, st, m_old, acc_old):
        m_new = jnp.maximum(m_old, jnp.max(st, axis=0, keepdims=True))
        p = jnp.exp2(st - m_new).astype(jnp.bfloat16)
        alpha = jnp.exp2(m_old - m_new)
        vt = vt_ref[0, hd * VT_ROWS:(hd + 1) * VT_ROWS, pl.ds(k0, sub)]
        return m_new, alpha * acc_old + jnp.dot(vt, p, preferred_element_type=jnp.float32)

    def step(slot, k0, nxt_q0, nxt_k0, diag):
        for static_slot in range(2):
            @pl.when(slot == static_slot)
            def _():
                static_step(static_slot, k0, nxt_q0, nxt_k0, diag)

    def static_step(slot, k0, nxt_q0, nxt_k0, diag):
        state = {(hd, qg): (m_ref[hd, :, qg * sub:(qg + 1) * sub],
                            acc_ref[hd, :, qg * sub:(qg + 1) * sub])
                 for hd in range(heads) for qg in range(groups)}

        def issue(n):
            ks, hd, qg = tiles[n]
            pend_ref[1 - slot, n] = scores(hd, qg, nxt_q0, nxt_k0 + ks * sub)

        for n in range(min(ATTN_ISSUE_AHEAD, len(tiles))):
            issue(n)
        for n, (ks, hd, qg) in enumerate(tiles):
            if n + ATTN_ISSUE_AHEAD < len(tiles):
                issue(n + ATTN_ISSUE_AHEAD)
            if diag and qg < ks:
                continue
            st = pend_ref[slot, n]
            if diag and qg == ks:
                key = lax.broadcasted_iota(jnp.int32, (sub, sub), 0)
                qry = lax.broadcasted_iota(jnp.int32, (sub, sub), 1)
                st = jnp.where(key <= qry, st, NEG_BIG)
            state[hd, qg] = update(hd, k0 + ks * sub, st, *state[hd, qg])
        for (hd, qg), (m, acc) in state.items():
            m_ref[hd, :, qg * sub:(qg + 1) * sub] = m
            acc_ref[hd, :, qg * sub:(qg + 1) * sub] = acc

    def q_block(j, slot):
        q0 = pl.multiple_of(j * blk, blk)
        m_ref[...] = jnp.full(m_ref.shape, NEG_BIG, jnp.float32)
        acc_ref[...] = jnp.zeros(acc_ref.shape, jnp.float32)

        def k_block(i, slot):
            k0 = pl.multiple_of(i * blk, blk)
            step(slot, k0, q0, k0 + blk, diag=False)
            return 1 - slot

        slot = lax.fori_loop(0, j, k_block, slot)
        nxt_q0 = pl.multiple_of(jnp.minimum(j + 1, n_q - 1) * blk, blk)
        step(slot, q0, nxt_q0, 0, diag=True)

        for hd in range(heads):
            acc = acc_ref[hd]
            out = acc[:HEAD_DIM, :] / acc[HEAD_DIM:HEAD_DIM + 1, :]
            o_ref[0, hd * HEAD_DIM:(hd + 1) * HEAD_DIM, pl.ds(q0, blk)] = out.astype(o_ref.dtype)
        return 1 - slot

    for n, (ks, hd, qg) in enumerate(tiles):
        pend_ref[0, n] = scores(hd, qg, 0, ks * sub)
    lax.fori_loop(0, n_q, q_block, 0)


def _fox_attn(qp, kp, vtp):
    B, S, _ = qp.shape
    hp = ATTN_HEADS_PER_STEP
    return pl.pallas_call(
        _fox_attn_kernel,
        grid=(B, N_HEADS // hp),
        in_specs=[
            pl.BlockSpec((1, S, hp * HEAD_SLOT), lambda b, h: (b, 0, h)),
            pl.BlockSpec((1, S, hp * HEAD_SLOT), lambda b, h: (b, 0, h)),
            pl.BlockSpec((1, hp * VT_ROWS, S), lambda b, h: (b, h, 0)),
        ],
        out_specs=pl.BlockSpec((1, hp * HEAD_DIM, S), lambda b, h: (b, h, 0)),
        out_shape=jax.ShapeDtypeStruct((B, ATTN_WIDTH, S), jnp.bfloat16),
        scratch_shapes=[pltpu.VMEM((hp, 1, ATTN_BLOCK), jnp.float32),
                        pltpu.VMEM((hp, VT_ROWS, ATTN_BLOCK), jnp.float32),
                        pltpu.VMEM((2, hp * (ATTN_BLOCK // ATTN_UNIT) ** 2, ATTN_UNIT, ATTN_UNIT),
                                   jnp.float32)],
        compiler_params=pltpu.CompilerParams(
            dimension_semantics=("arbitrary", "arbitrary"), vmem_limit_bytes=VMEM_LIMIT),
        name="fox_attn",
    )(qp, kp, vtp)


def _mix_ffn_kernel(x_ref, at_ref, u_ref, uprev_ref, wpool_ref, pscale_ref, wout_ref, g2_ref,
                    wg_ref, wu_ref, wd_ref, gf_ref, y_ref, ext_ref, stage_ref, *, final_norm):
    tm = x_ref.shape[1]
    j = pl.program_id(1)

    n_groups = len(POOL_WINDOWS)
    out_chunk = D_MODEL // n_groups

    pad, top = POOL_PAD, POOL_PAD + POOL_HALO
    u = u_ref[0]
    ext_ref[:pad, :] = jnp.zeros((pad, POOL_WIDTH), jnp.float32)
    ext_ref[pad:top, :] = jnp.where(j > 0, uprev_ref[0], 0.0)
    ext_ref[top:, :] = u
    stage_ref[:, :pad, :] = jnp.zeros((stage_ref.shape[0], pad, POOL_GROUP), jnp.float32)

    first_rows = lax.broadcasted_iota(jnp.int32, (POOL_HALO, POOL_GROUP), 0) + 1
    mixed, x1_cols = [], []
    for gi, w in enumerate(POOL_WINDOWS):
        ocols = slice(gi * out_chunk, (gi + 1) * out_chunk)
        x1_cols.append(x_ref[0, :, ocols]
                       + lax.dot_general(at_ref[0], wout_ref[:ATTN_WIDTH, ocols], _TN,
                                         preferred_element_type=jnp.float32))
        cols = slice(gi * POOL_GROUP, (gi + 1) * POOL_GROUP)
        end = ext_ref.shape[0]
        wsum = ext_ref[pad:end, cols] + ext_ref[pad - 1:end - 1, cols]
        span = 2
        while span < w:
            level = stage_ref.at[(span.bit_length() - 2) % stage_ref.shape[0]]
            level[pad:end, :] = wsum
            wsum = wsum + level[pad - span:end - span, :]
            span *= 2
        wsum = wsum[POOL_HALO:, :]
        mean = wsum * (1.0 / w)
        exact_head = wsum[:POOL_HALO, :] / jnp.minimum(first_rows, w).astype(jnp.float32)
        head = jnp.where(j == 0, exact_head, mean[:POOL_HALO, :])
        mean = jnp.concatenate([head, mean[POOL_HALO:, :]], axis=0)
        pooled = (mean - u[:, cols]).astype(jnp.bfloat16)
        mg = jnp.dot(pooled, wpool_ref[gi], preferred_element_type=jnp.float32)
        mixed.append((mg * pscale_ref[:, cols]).astype(jnp.bfloat16))
    pool = jnp.concatenate(mixed, axis=1)
    x1 = jnp.concatenate(x1_cols, axis=1) + jnp.dot(pool, wout_ref[ATTN_WIDTH:, :],
                                                    preferred_element_type=jnp.float32)

    h2 = _rms(x1, g2_ref[...]).astype(jnp.bfloat16)
    ffn = jnp.zeros((tm, D_MODEL), jnp.float32)
    for c0 in range(0, D_FF, FF_CHUNK):
        gate = jnp.dot(h2, wg_ref[:, c0:c0 + FF_CHUNK], preferred_element_type=jnp.float32)
        up = jnp.dot(h2, wu_ref[:, c0:c0 + FF_CHUNK], preferred_element_type=jnp.float32)
        act = (gate / (1.0 + jnp.exp(-gate)) * up).astype(jnp.bfloat16)
        ffn = ffn + jnp.dot(act, wd_ref[c0:c0 + FF_CHUNK, :], preferred_element_type=jnp.float32)

    x2 = x1 + ffn
    y_ref[0] = _rms(x2, gf_ref[...]) if final_norm else x2


def _mix_ffn(x, attn_t, u, wpool, pscale, wout, g2, wg, wu, wd, gf, final_norm):
    B, S, D = x.shape
    tm = TOKEN_TILE
    halo_blocks = tm // POOL_HALO
    const = lambda shape: pl.BlockSpec(shape, lambda b, j: (0,) * len(shape),
                                       pipeline_mode=pl.Buffered(1))
    return pl.pallas_call(
        functools.partial(_mix_ffn_kernel, final_norm=final_norm),
        grid=(B, S // tm),
        in_specs=[
            pl.BlockSpec((1, tm, D), lambda b, j: (b, j, 0)),
            pl.BlockSpec((1, ATTN_WIDTH, tm), lambda b, j: (b, 0, j)),
            pl.BlockSpec((1, tm, POOL_WIDTH), lambda b, j: (b, j, 0)),
            pl.BlockSpec((1, POOL_HALO, POOL_WIDTH),
                         lambda b, j: (b, jnp.maximum(j * halo_blocks - 1, 0), 0)),
            const(wpool.shape), const(pscale.shape), const(wout.shape), const(g2.shape),
            const(wg.shape), const(wu.shape), const(wd.shape), const(gf.shape),
        ],
        out_specs=pl.BlockSpec((1, tm, D), lambda b, j: (b, j, 0)),
        out_shape=jax.ShapeDtypeStruct((B, S, D), jnp.float32),
        scratch_shapes=[pltpu.VMEM((POOL_PAD + POOL_HALO + tm, POOL_WIDTH), jnp.float32),
                        pltpu.VMEM((2, POOL_PAD + POOL_HALO + tm, POOL_GROUP), jnp.float32)],
        compiler_params=pltpu.CompilerParams(
            dimension_semantics=("arbitrary", "arbitrary"), vmem_limit_bytes=VMEM_LIMIT),
        name="mix_ffn",
    )(x, attn_t, u, u, wpool, pscale, wout, g2, wg, wu, wd, gf)


def kernel(x, norm1_g, w_in, b_forget, w_pool, pool_scale, w_out, norm2_g, w_gate, w_up, w_down, final_g):
    depth = w_in.shape[0]
    bf16 = jnp.bfloat16
    a0 = ATTN_WIDTH
    n_bias = N_BIAS_TERMS * N_HEADS
    lane = np.arange(LANES)
    mod3 = jnp.asarray(np.where(lane < n_bias, lane % N_BIAS_TERMS, N_BIAS_TERMS)[None, :], jnp.int32)
    for layer in range(depth):
        w = w_in[layer]
        wqk = w[:, :2 * a0].astype(bf16)
        wvt = w[:, 2 * a0:3 * a0].T.astype(bf16)
        wf = jnp.repeat(w[:, 3 * a0:3 * a0 + N_HEADS], N_BIAS_TERMS, axis=1)
        wuf = jnp.concatenate([w[:, 3 * a0 + N_HEADS:], wf,
                               jnp.zeros((D_MODEL, LANES - n_bias), w.dtype)], axis=1).astype(bf16)
        bf = jnp.concatenate([jnp.repeat(b_forget[layer], N_BIAS_TERMS),
                              jnp.zeros((LANES - n_bias,), b_forget.dtype)]).astype(jnp.float32)[None, :]
        (qp, kp, vtp, u), (wg, wu, wd) = _in_proj(
            x, norm1_g[layer][None, :], wqk, wvt, wuf, bf, mod3,
            (w_gate[layer], w_up[layer], w_down[layer]))
        attn_t = _fox_attn(qp, kp, vtp)
        x = _mix_ffn(x, attn_t, u, w_pool[layer].astype(bf16), pool_scale[layer][None, :],
                     w_out[layer].astype(bf16), norm2_g[layer][None, :], wg, wu, wd,
                     final_g[None, :], final_norm=layer == depth - 1)
    return x
```

```python
import functools
import math

import jax
import jax.numpy as jnp
import numpy as np
from jax import lax
from jax.experimental import pallas as pl
from jax.experimental.pallas import tpu as pltpu

D_MODEL = 1024
ATTN_WIDTH = 512
N_HEADS = 8
HEAD_DIM = 64
POOL_WIDTH = 512
POOL_WINDOWS = (2, 4, 8, 16)
POOL_GROUP = 128
D_FF = 2816
EPS = 1e-6

LANES = 128
BF16_SUBLANES = 16
HEAD_SLOT = 128
N_BIAS_TERMS = 3
VT_ROWS = HEAD_DIM + 16
POOL_HALO = 16
POOL_PAD = 8
LOG2E = math.log2(math.e)
NEG_BIG = -1e30

TOKEN_TILE = 1024
IN_PROJ_TILE = 1024
ATTN_BLOCK = 512
ATTN_UNIT = 256
ATTN_HEADS_PER_STEP = 4
ATTN_ISSUE_AHEAD = 2
FF_CHUNK = 256
VMEM_LIMIT = 56 * 1024 * 1024

_NT = (((1,), (1,)), ((), ()))
_TN = (((0,), (0,)), ((), ()))


def _rms(x, g):
    ms = jnp.mean(x * x, axis=-1, keepdims=True)
    return x * lax.rsqrt(ms + EPS) * g


def _split_bf16(x):
    hi = x.astype(jnp.bfloat16)
    r1 = x - hi.astype(jnp.float32)
    mid = r1.astype(jnp.bfloat16)
    lo = (r1 - mid.astype(jnp.float32)).astype(jnp.bfloat16)
    return hi, mid, lo


def _in_proj_kernel(x_ref, g_ref, wqk_ref, wv_ref, wuf_ref, bf_ref, mod3_ref,
                    wg_ref, wu_ref, wd_ref,
                    qp_ref, kp_ref, vt_ref, u_ref, wg16_ref, wu16_ref, wd16_ref,
                    carry_ref, scan_ref):
    tm = x_ref.shape[1]
    pad = scan_ref.shape[1] - tm

    @pl.when(pl.program_id(1) == 0)
    def _():
        carry_ref[...] = jnp.zeros_like(carry_ref)


    h = _rms(x_ref[0], g_ref[...]).astype(jnp.bfloat16)
    uf = jnp.dot(h, wuf_ref[...], preferred_element_type=jnp.float32)
    u_ref[0] = uf[:, :POOL_WIDTH]
    f = uf[:, POOL_WIDTH:] + bf_ref[...]
    vt = jnp.dot(h, wv_ref[...], preferred_element_type=jnp.float32).T
    qk = jnp.dot(h, wqk_ref[...], preferred_element_type=jnp.float32)

    wg16_ref[...] = wg_ref[...].astype(wg16_ref.dtype)
    wu16_ref[...] = wu_ref[...].astype(wu16_ref.dtype)
    wd16_ref[...] = wd_ref[...].astype(wd16_ref.dtype)

    lf = jnp.minimum(f, 0.0) - jnp.log(1.0 + jnp.exp(-jnp.abs(f)))
    scan_ref[:, :pad, :] = jnp.zeros((scan_ref.shape[0], pad, LANES), jnp.float32)
    shift, level = 1, 0
    while shift < tm:
        scan_ref[level, pad:, :] = lf
        lf = lf + scan_ref[level, pad - shift:pad - shift + tm, :]
        shift, level = 2 * shift, 1 - level
    c = lf + carry_ref[...]
    carry_ref[...] = c[tm - 1:tm, :]

    hi, mid, lo = _split_bf16(c * (-LOG2E))
    m3 = mod3_ref[...]
    e3 = jnp.where(m3 == 0, hi.astype(jnp.float32),
                   jnp.where(m3 == 1, mid.astype(jnp.float32), lo.astype(jnp.float32)))

    lane = lax.broadcasted_iota(jnp.int32, (tm, LANES), 1)
    low_half = lane < HEAD_DIM
    ones_hi = jnp.where((lane >= HEAD_DIM) & (lane < HEAD_DIM + N_BIAS_TERMS), 1.0, 0.0)
    ones_lo = jnp.where(lane < N_BIAS_TERMS, 1.0, 0.0)
    q_scale = HEAD_DIM ** -0.5 * LOG2E
    for hd in range(N_HEADS):
        pair = hd // 2
        qpair = qk[:, pair * LANES:(pair + 1) * LANES] * q_scale
        kpair = qk[:, ATTN_WIDTH + pair * LANES:ATTN_WIDTH + (pair + 1) * LANES]
        if hd % 2 == 0:
            bias = pltpu.roll(e3, HEAD_DIM - N_BIAS_TERMS * hd, 1)
            qh = jnp.where(low_half, qpair, ones_hi)
            kh = jnp.where(low_half, kpair, bias)
        else:
            bias = pltpu.roll(e3, LANES - N_BIAS_TERMS * hd, 1)
            qh = jnp.where(low_half, ones_lo, qpair)
            kh = jnp.where(low_half, bias, kpair)
        qp_ref[0, :, hd * HEAD_SLOT:(hd + 1) * HEAD_SLOT] = qh.astype(jnp.bfloat16)
        kp_ref[0, :, hd * HEAD_SLOT:(hd + 1) * HEAD_SLOT] = kh.astype(jnp.bfloat16)
        vt_ref[0, hd * VT_ROWS:hd * VT_ROWS + HEAD_DIM, :] = (
            vt[hd * HEAD_DIM:(hd + 1) * HEAD_DIM, :].astype(jnp.bfloat16))
        vt_ref[0, hd * VT_ROWS + HEAD_DIM:(hd + 1) * VT_ROWS, :] = jnp.ones(
            (VT_ROWS - HEAD_DIM, tm), jnp.bfloat16)


def _in_proj(x, g, wqk, wv, wuf, bf, mod3, ffn_weights):
    B, S, D = x.shape
    tm = IN_PROJ_TILE
    nj = S // tm
    n_steps = B * nj
    const = lambda shape: pl.BlockSpec(shape, lambda b, j: (0,) * len(shape))

    def slab_spec(w):
        rows = w.shape[0]
        n_slabs = max(k for k in range(1, n_steps + 1)
                      if rows % k == 0 and (rows // k) % BF16_SUBLANES == 0)
        return pl.BlockSpec((rows // n_slabs, w.shape[1]),
                            lambda b, j: (jnp.minimum(b * nj + j, n_slabs - 1), 0))

    outs = pl.pallas_call(
        _in_proj_kernel,
        grid=(B, nj),
        in_specs=[
            pl.BlockSpec((1, tm, D), lambda b, j: (b, j, 0)),
            const(g.shape), const(wqk.shape), const(wv.shape), const(wuf.shape),
            const(bf.shape), const(mod3.shape),
        ] + [slab_spec(w) for w in ffn_weights],
        out_specs=[
            pl.BlockSpec((1, tm, N_HEADS * HEAD_SLOT), lambda b, j: (b, j, 0)),
            pl.BlockSpec((1, tm, N_HEADS * HEAD_SLOT), lambda b, j: (b, j, 0)),
            pl.BlockSpec((1, N_HEADS * VT_ROWS, tm), lambda b, j: (b, 0, j)),
            pl.BlockSpec((1, tm, POOL_WIDTH), lambda b, j: (b, j, 0)),
        ] + [slab_spec(w) for w in ffn_weights],
        out_shape=[
            jax.ShapeDtypeStruct((B, S, N_HEADS * HEAD_SLOT), jnp.bfloat16),
            jax.ShapeDtypeStruct((B, S, N_HEADS * HEAD_SLOT), jnp.bfloat16),
            jax.ShapeDtypeStruct((B, N_HEADS * VT_ROWS, S), jnp.bfloat16),
            jax.ShapeDtypeStruct((B, S, POOL_WIDTH), jnp.float32),
        ] + [jax.ShapeDtypeStruct(w.shape, jnp.bfloat16) for w in ffn_weights],
        scratch_shapes=[pltpu.VMEM((1, LANES), jnp.float32),
                        pltpu.VMEM((2, tm // 2 + tm, LANES), jnp.float32)],
        compiler_params=pltpu.CompilerParams(
            dimension_semantics=("arbitrary", "arbitrary"), vmem_limit_bytes=VMEM_LIMIT),
        name="in_proj",
    )(x, g, wqk, wv, wuf, bf, mod3, *ffn_weights)
    return outs[:4], outs[4:]


def _fox_attn_kernel(q_ref, k_ref, vt_ref, o_ref, m_ref, acc_ref, pend_ref):
    S = q_ref.shape[1]
    blk, sub = ATTN_BLOCK, ATTN_UNIT
    heads = q_ref.shape[2] // HEAD_SLOT
    groups = blk // sub
    n_q = S // blk
    tiles = [(ks, hd, qg) for ks in range(groups) for hd in range(heads) for qg in range(groups)]

    def scores(hd, qg, q0, k0):
        lanes = slice(hd * HEAD_SLOT, (hd + 1) * HEAD_SLOT)
        qt = q_ref[0, pl.ds(q0 + qg * sub, sub), lanes]
        kt = k_ref[0, pl.ds(k0, sub), lanes]
        return lax.dot_general(kt, qt, _NT, preferred_element_type=jnp.float32)

    def update(hd, k0, st, m_old, acc_old):
        m_new = jnp.maximum(m_old, jnp.max(st, axis=0, keepdims=True))
        p = jnp.exp2(st - m_new).astype(jnp.bfloat16)
        alpha = jnp.exp2(m_old - m_new)
        vt = vt_ref[0, hd * VT_ROWS:(hd + 1) * VT_ROWS, pl.ds(k0, sub)]
        return m_new, alpha * acc_old + jnp.dot(vt, p, preferred_element_type=jnp.float32)

    def step(slot, k0, nxt_q0, nxt_k0, diag):
        for static_slot in range(2):
            @pl.when(slot == static_slot)
            def _():
                static_step(static_slot, k0, nxt_q0, nxt_k0, diag)

    def static_step(slot, k0, nxt_q0, nxt_k0, diag):
        state = {(hd, qg): (m_ref[hd, :, qg * sub:(qg + 1) * sub],
                            acc_ref[hd, :, qg * sub:(qg + 1) * sub])
                 for hd in range(heads) for qg in range(groups)}

        def issue(n):
            ks, hd, qg = tiles[n]
            pend_ref[1 - slot, n] = scores(hd, qg, nxt_q0, nxt_k0 + ks * sub)

        for n in range(min(ATTN_ISSUE_AHEAD, len(tiles))):
            issue(n)
        for n, (ks, hd, qg) in enumerate(tiles):
            if n + ATTN_ISSUE_AHEAD < len(tiles):
                issue(n + ATTN_ISSUE_AHEAD)
            if diag and qg < ks:
                continue
            st = pend_ref[slot, n]
            if diag and qg == ks:
                key = lax.broadcasted_iota(jnp.int32, (sub, sub), 0)
                qry = lax.broadcasted_iota(jnp.int32, (sub, sub), 1)
                st = jnp.where(key <= qry, st, NEG_BIG)
            state[hd, qg] = update(hd, k0 + ks * sub, st, *state[hd, qg])
        for (hd, qg), (m, acc) in state.items():
            m_ref[hd, :, qg * sub:(qg + 1) * sub] = m
            acc_ref[hd, :, qg * sub:(qg + 1) * sub] = acc

    def q_block(j, slot):
        q0 = pl.multiple_of(j * blk, blk)
        m_ref[...] = jnp.full(m_ref.shape, NEG_BIG, jnp.float32)
        acc_ref[...] = jnp.zeros(acc_ref.shape, jnp.float32)

        def k_block(i, slot):
            k0 = pl.multiple_of(i * blk, blk)
            step(slot, k0, q0, k0 + blk, diag=False)
            return 1 - slot

        slot = lax.fori_loop(0, j, k_block, slot)
        nxt_q0 = pl.multiple_of(jnp.minimum(j + 1, n_q - 1) * blk, blk)
        step(slot, q0, nxt_q0, 0, diag=True)

        for hd in range(heads):
            acc = acc_ref[hd]
            out = acc[:HEAD_DIM, :] / acc[HEAD_DIM:HEAD_DIM + 1, :]
            o_ref[0, hd * HEAD_DIM:(hd + 1) * HEAD_DIM, pl.ds(q0, blk)] = out.astype(o_ref.dtype)
        return 1 - slot

    for n, (ks, hd, qg) in enumerate(tiles):
        pend_ref[0, n] = scores(hd, qg, 0, ks * sub)
    lax.fori_loop(0, n_q, q_block, 0)


def _fox_attn(qp, kp, vtp):
    B, S, _ = qp.shape
    hp = ATTN_HEADS_PER_STEP
    return pl.pallas_call(
        _fox_attn_kernel,
        grid=(B, N_HEADS // hp),
        in_specs=[
            pl.BlockSpec((1, S, hp * HEAD_SLOT), lambda b, h: (b, 0, h)),
            pl.BlockSpec((1, S, hp * HEAD_SLOT), lambda b, h: (b, 0, h)),
            pl.BlockSpec((1, hp * VT_ROWS, S), lambda b, h: (b, h, 0)),
        ],
        out_specs=pl.BlockSpec((1, hp * HEAD_DIM, S), lambda b, h: (b, h, 0)),
        out_shape=jax.ShapeDtypeStruct((B, ATTN_WIDTH, S), jnp.bfloat16),
        scratch_shapes=[pltpu.VMEM((hp, 1, ATTN_BLOCK), jnp.float32),
                        pltpu.VMEM((hp, VT_ROWS, ATTN_BLOCK), jnp.float32),
                        pltpu.VMEM((2, hp * (ATTN_BLOCK // ATTN_UNIT) ** 2, ATTN_UNIT, ATTN_UNIT),
                                   jnp.float32)],
        compiler_params=pltpu.CompilerParams(
            dimension_semantics=("arbitrary", "arbitrary"), vmem_limit_bytes=VMEM_LIMIT),
        name="fox_attn",
    )(qp, kp, vtp)


def _mix_ffn_kernel(x_ref, at_ref, u_ref, uprev_ref, wpool_ref, pscale_ref, wout_ref, g2_ref,
                    wg_ref, wu_ref, wd_ref, gf_ref, y_ref, ext_ref, stage_ref, *, final_norm):
    tm = x_ref.shape[1]
    j = pl.program_id(1)

    n_groups = len(POOL_WINDOWS)
    out_chunk = D_MODEL // n_groups

    pad, top = POOL_PAD, POOL_PAD + POOL_HALO
    u = u_ref[0]
    ext_ref[:pad, :] = jnp.zeros((pad, POOL_WIDTH), jnp.float32)
    ext_ref[pad:top, :] = jnp.where(j > 0, uprev_ref[0], 0.0)
    ext_ref[top:, :] = u
    stage_ref[:, :pad, :] = jnp.zeros((stage_ref.shape[0], pad, POOL_GROUP), jnp.float32)

    first_rows = lax.broadcasted_iota(jnp.int32, (POOL_HALO, POOL_GROUP), 0) + 1
    mixed, x1_cols = [], []
    for gi, w in enumerate(POOL_WINDOWS):
        ocols = slice(gi * out_chunk, (gi + 1) * out_chunk)
        x1_cols.append(x_ref[0, :, ocols]
                       + lax.dot_general(at_ref[0], wout_ref[:ATTN_WIDTH, ocols], _TN,
                                         preferred_element_type=jnp.float32))
        cols = slice(gi * POOL_GROUP, (gi + 1) * POOL_GROUP)
        end = ext_ref.shape[0]
        wsum = ext_ref[pad:end, cols] + ext_ref[pad - 1:end - 1, cols]
        span = 2
        while span < w:
            level = stage_ref.at[(span.bit_length() - 2) % stage_ref.shape[0]]
            level[pad:end, :] = wsum
            wsum = wsum + level[pad - span:end - span, :]
            span *= 2
        wsum = wsum[POOL_HALO:, :]
        mean = wsum * (1.0 / w)
        exact_head = wsum[:POOL_HALO, :] / jnp.minimum(first_rows, w).astype(jnp.float32)
        head = jnp.where(j == 0, exact_head, mean[:POOL_HALO, :])
        mean = jnp.concatenate([head, mean[POOL_HALO:, :]], axis=0)
        pooled = (mean - u[:, cols]).astype(jnp.bfloat16)
        mg = jnp.dot(pooled, wpool_ref[gi], preferred_element_type=jnp.float32)
        mixed.append((mg * pscale_ref[:, cols]).astype(jnp.bfloat16))
    pool = jnp.concatenate(mixed, axis=1)
    x1 = jnp.concatenate(x1_cols, axis=1) + jnp.dot(pool, wout_ref[ATTN_WIDTH:, :],
                                                    preferred_element_type=jnp.float32)

    h2 = _rms(x1, g2_ref[...]).astype(jnp.bfloat16)
    ffn = jnp.zeros((tm, D_MODEL), jnp.float32)
    for c0 in range(0, D_FF, FF_CHUNK):
        gate = jnp.dot(h2, wg_ref[:, c0:c0 + FF_CHUNK], preferred_element_type=jnp.float32)
        up = jnp.dot(h2, wu_ref[:, c0:c0 + FF_CHUNK], preferred_element_type=jnp.float32)
        act = (gate / (1.0 + jnp.exp(-gate)) * up).astype(jnp.bfloat16)
        ffn = ffn + jnp.dot(act, wd_ref[c0:c0 + FF_CHUNK, :], preferred_element_type=jnp.float32)

    x2 = x1 + ffn
    y_ref[0] = _rms(x2, gf_ref[...]) if final_norm else x2


def _mix_ffn(x, attn_t, u, wpool, pscale, wout, g2, wg, wu, wd, gf, final_norm):
    B, S, D = x.shape
    tm = TOKEN_TILE
    halo_blocks = tm // POOL_HALO
    const = lambda shape: pl.BlockSpec(shape, lambda b, j: (0,) * len(shape),
                                       pipeline_mode=pl.Buffered(1))
    return pl.pallas_call(
        functools.partial(_mix_ffn_kernel, final_norm=final_norm),
        grid=(B, S // tm),
        in_specs=[
            pl.BlockSpec((1, tm, D), lambda b, j: (b, j, 0)),
            pl.BlockSpec((1, ATTN_WIDTH, tm), lambda b, j: (b, 0, j)),
            pl.BlockSpec((1, tm, POOL_WIDTH), lambda b, j: (b, j, 0)),
            pl.BlockSpec((1, POOL_HALO, POOL_WIDTH),
                         lambda b, j: (b, jnp.maximum(j * halo_blocks - 1, 0), 0)),
            const(wpool.shape), const(pscale.shape), const(wout.shape), const(g2.shape),
            const(wg.shape), const(wu.shape), const(wd.shape), const(gf.shape),
        ],
        out_specs=pl.BlockSpec((1, tm, D), lambda b, j: (b, j, 0)),
        out_shape=jax.ShapeDtypeStruct((B, S, D), jnp.float32),
        scratch_shapes=[pltpu.VMEM((POOL_PAD + POOL_HALO + tm, POOL_WIDTH), jnp.float32),
                        pltpu.VMEM((2, POOL_PAD + POOL_HALO + tm, POOL_GROUP), jnp.float32)],
        compiler_params=pltpu.CompilerParams(
            dimension_semantics=("arbitrary", "arbitrary"), vmem_limit_bytes=VMEM_LIMIT),
        name="mix_ffn",
    )(x, attn_t, u, u, wpool, pscale, wout, g2, wg, wu, wd, gf)


def kernel(x, norm1_g, w_in, b_forget, w_pool, pool_scale, w_out, norm2_g, w_gate, w_up, w_down, final_g):
    depth = w_in.shape[0]
    bf16 = jnp.bfloat16
    a0 = ATTN_WIDTH
    n_bias = N_BIAS_TERMS * N_HEADS
    lane = np.arange(LANES)
    mod3 = jnp.asarray(np.where(lane < n_bias, lane % N_BIAS_TERMS, N_BIAS_TERMS)[None, :], jnp.int32)
    for layer in range(depth):
        w = w_in[layer]
        wqk = w[:, :2 * a0].astype(bf16)
        wv = w[:, 2 * a0:3 * a0].astype(bf16)
        wf = jnp.repeat(w[:, 3 * a0:3 * a0 + N_HEADS], N_BIAS_TERMS, axis=1)
        wuf = jnp.concatenate([w[:, 3 * a0 + N_HEADS:], wf,
                               jnp.zeros((D_MODEL, LANES - n_bias), w.dtype)], axis=1).astype(bf16)
        bf = jnp.concatenate([jnp.repeat(b_forget[layer], N_BIAS_TERMS),
                              jnp.zeros((LANES - n_bias,), b_forget.dtype)]).astype(jnp.float32)[None, :]
        (qp, kp, vtp, u), (wg, wu, wd) = _in_proj(
            x, norm1_g[layer][None, :], wqk, wv, wuf, bf, mod3,
            (w_gate[layer], w_up[layer], w_down[layer]))
        attn_t = _fox_attn(qp, kp, vtp)
        x = _mix_ffn(x, attn_t, u, w_pool[layer].astype(bf16), pool_scale[layer][None, :],
                     w_out[layer].astype(bf16), norm2_g[layer][None, :], wg, wu, wd,
                     final_g[None, :], final_norm=layer == depth - 1)
    return x
```

```python
import functools
import math

import jax
import jax.numpy as jnp
import numpy as np
from jax import lax
from jax.experimental import pallas as pl
from jax.experimental.pallas import tpu as pltpu

D_MODEL = 1024
ATTN_WIDTH = 512
N_HEADS = 8
HEAD_DIM = 64
POOL_WIDTH = 512
POOL_WINDOWS = (2, 4, 8, 16)
POOL_GROUP = 128
D_FF = 2816
EPS = 1e-6

LANES = 128
BF16_SUBLANES = 16
HEAD_SLOT = 128
N_BIAS_TERMS = 3
VT_ROWS = HEAD_DIM + BF16_SUBLANES
POOL_HALO = 16
POOL_PAD = 8
LOG2E = math.log2(math.e)
NEG_BIG = -1e30

TOKEN_TILE = 1024
IN_PROJ_TILE = 1024
ATTN_BLOCK = 512
ATTN_UNIT = 256
ATTN_HEADS_PER_STEP = 4
ATTN_ISSUE_AHEAD = 2
FF_CHUNK = 256
VMEM_LIMIT = 56 * 1024 * 1024

_NT = (((1,), (1,)), ((), ()))
_TN = (((0,), (0,)), ((), ()))


def _rms(x, g):
    ms = jnp.mean(x * x, axis=-1, keepdims=True)
    return x * lax.rsqrt(ms + EPS) * g


def _split_bf16(x):
    hi = x.astype(jnp.bfloat16)
    r1 = x - hi.astype(jnp.float32)
    mid = r1.astype(jnp.bfloat16)
    lo = (r1 - mid.astype(jnp.float32)).astype(jnp.bfloat16)
    return hi, mid, lo


def _in_proj_kernel(x_ref, g_ref, w_ref, bf_ref, mod3_ref, *rest, n_cast):
    cast_in = rest[:n_cast]
    qp_ref, kp_ref, vt_ref, u_ref = rest[n_cast:n_cast + 4]
    cast_out = rest[n_cast + 4:2 * n_cast + 4]
    carry_ref, scan_ref = rest[2 * n_cast + 4:]
    tm = x_ref.shape[1]
    pad = scan_ref.shape[1] - tm

    @pl.when(pl.program_id(1) == 0)
    def _():
        carry_ref[...] = jnp.zeros_like(carry_ref)

    h = _rms(x_ref[0], g_ref[...]).astype(jnp.bfloat16)
    qk_end, v_end = 2 * ATTN_WIDTH, 3 * ATTN_WIDTH
    uf = jnp.dot(h, w_ref[:, v_end:], preferred_element_type=jnp.float32)
    u_ref[0] = uf[:, :POOL_WIDTH]
    f = uf[:, POOL_WIDTH:] + bf_ref[...]
    vt = jnp.dot(h, w_ref[:, qk_end:v_end], preferred_element_type=jnp.float32).T
    qk = jnp.dot(h, w_ref[:, :qk_end], preferred_element_type=jnp.float32)

    for src, dst in zip(cast_in, cast_out):
        dst[...] = src[...].astype(dst.dtype)

    lf = jnp.minimum(f, 0.0) - jnp.log(1.0 + jnp.exp(-jnp.abs(f)))
    scan_ref[:, :pad, :] = jnp.zeros((scan_ref.shape[0], pad, LANES), jnp.float32)
    shift, level = 1, 0
    while shift < tm:
        scan_ref[level, pad:, :] = lf
        lf = lf + scan_ref[level, pad - shift:pad - shift + tm, :]
        shift, level = 2 * shift, 1 - level
    c = lf + carry_ref[...]
    carry_ref[...] = c[tm - 1:tm, :]

    hi, mid, lo = _split_bf16(c * (-LOG2E))
    m3 = mod3_ref[...]
    e3 = jnp.where(m3 == 0, hi.astype(jnp.float32),
                   jnp.where(m3 == 1, mid.astype(jnp.float32), lo.astype(jnp.float32)))

    lane = lax.broadcasted_iota(jnp.int32, (tm, LANES), 1)
    low_half = lane < HEAD_DIM
    ones_hi = jnp.where((lane >= HEAD_DIM) & (lane < HEAD_DIM + N_BIAS_TERMS), 1.0, 0.0)
    ones_lo = jnp.where(lane < N_BIAS_TERMS, 1.0, 0.0)
    q_scale = HEAD_DIM ** -0.5 * LOG2E
    for hd in range(N_HEADS):
        pair = hd // 2
        qpair = qk[:, pair * LANES:(pair + 1) * LANES] * q_scale
        kpair = qk[:, ATTN_WIDTH + pair * LANES:ATTN_WIDTH + (pair + 1) * LANES]
        if hd % 2 == 0:
            bias = pltpu.roll(e3, HEAD_DIM - N_BIAS_TERMS * hd, 1)
            qh = jnp.where(low_half, qpair, ones_hi)
            kh = jnp.where(low_half, kpair, bias)
        else:
            bias = pltpu.roll(e3, LANES - N_BIAS_TERMS * hd, 1)
            qh = jnp.where(low_half, ones_lo, qpair)
            kh = jnp.where(low_half, bias, kpair)
        qp_ref[0, :, hd * HEAD_SLOT:(hd + 1) * HEAD_SLOT] = qh.astype(jnp.bfloat16)
        kp_ref[0, :, hd * HEAD_SLOT:(hd + 1) * HEAD_SLOT] = kh.astype(jnp.bfloat16)
        vt_ref[0, hd * VT_ROWS:hd * VT_ROWS + HEAD_DIM, :] = (
            vt[hd * HEAD_DIM:(hd + 1) * HEAD_DIM, :].astype(jnp.bfloat16))
        vt_ref[0, hd * VT_ROWS + HEAD_DIM:(hd + 1) * VT_ROWS, :] = jnp.ones(
            (VT_ROWS - HEAD_DIM, tm), jnp.bfloat16)


def _in_proj(x, g, w_all, bf, mod3, ffn_weights):
    B, S, D = x.shape
    tm = IN_PROJ_TILE
    nj = S // tm
    n_steps = B * nj
    const = lambda shape: pl.BlockSpec(shape, lambda b, j: (0,) * len(shape))

    def slab_spec(w):
        rows = w.shape[0]
        n_slabs = max(k for k in range(1, n_steps + 1)
                      if rows % k == 0 and (rows // k) % BF16_SUBLANES == 0)
        return pl.BlockSpec((rows // n_slabs, w.shape[1]),
                            lambda b, j: (jnp.minimum(b * nj + j, n_slabs - 1), 0))

    outs = pl.pallas_call(
        functools.partial(_in_proj_kernel, n_cast=len(ffn_weights)),
        grid=(B, nj),
        in_specs=[
            pl.BlockSpec((1, tm, D), lambda b, j: (b, j, 0)),
            const(g.shape), const(w_all.shape), const(bf.shape), const(mod3.shape),
        ] + [slab_spec(w) for w in ffn_weights],
        out_specs=[
            pl.BlockSpec((1, tm, N_HEADS * HEAD_SLOT), lambda b, j: (b, j, 0)),
            pl.BlockSpec((1, tm, N_HEADS * HEAD_SLOT), lambda b, j: (b, j, 0)),
            pl.BlockSpec((1, N_HEADS * VT_ROWS, tm), lambda b, j: (b, 0, j)),
            pl.BlockSpec((1, tm, POOL_WIDTH), lambda b, j: (b, j, 0)),
        ] + [slab_spec(w) for w in ffn_weights],
        out_shape=[
            jax.ShapeDtypeStruct((B, S, N_HEADS * HEAD_SLOT), jnp.bfloat16),
            jax.ShapeDtypeStruct((B, S, N_HEADS * HEAD_SLOT), jnp.bfloat16),
            jax.ShapeDtypeStruct((B, N_HEADS * VT_ROWS, S), jnp.bfloat16),
            jax.ShapeDtypeStruct((B, S, POOL_WIDTH), jnp.float32),
        ] + [jax.ShapeDtypeStruct(w.shape, jnp.bfloat16) for w in ffn_weights],
        scratch_shapes=[pltpu.VMEM((1, LANES), jnp.float32),
                        pltpu.VMEM((2, tm // 2 + tm, LANES), jnp.float32)],
        compiler_params=pltpu.CompilerParams(
            dimension_semantics=("arbitrary", "arbitrary"), vmem_limit_bytes=VMEM_LIMIT),
        name="in_proj",
    )(x, g, w_all, bf, mod3, *ffn_weights)
    return outs[:4], outs[4:]


def _fox_attn_kernel(q_ref, k_ref, vt_ref, o_ref, m_ref, acc_ref, pend_ref):
    S = q_ref.shape[1]
    blk, sub = ATTN_BLOCK, ATTN_UNIT
    heads = q_ref.shape[2] // HEAD_SLOT
    groups = blk // sub
    n_q = S // blk
    tiles = [(ks, hd, qg) for ks in range(groups) for hd in range(heads) for qg in range(groups)]

    def scores(hd, qg, q0, k0):
        lanes = slice(hd * HEAD_SLOT, (hd + 1) * HEAD_SLOT)
        qt = q_ref[0, pl.ds(q0 + qg * sub, sub), lanes]
        kt = k_ref[0, pl.ds(k0, sub), lanes]
        return lax.dot_general(kt, qt, _NT, preferred_element_type=jnp.float32)

    def update(hd, k0, st, m_old, acc_old):
        m_new = jnp.maximum(m_old, jnp.max(st, axis=0, keepdims=True))
        p = jnp.exp2(st - m_new).astype(jnp.bfloat16)
        alpha = jnp.exp2(m_old - m_new)
        vt = vt_ref[0, hd * VT_ROWS:(hd + 1) * VT_ROWS, pl.ds(k0, sub)]
        return m_new, alpha * acc_old + jnp.dot(vt, p, preferred_element_type=jnp.float32)

    def step(slot, k0, nxt_q0, nxt_k0, diag, final=None):
        for static_slot in range(2):
            here = slot == static_slot
            if final is None:
                pl.when(here)(functools.partial(
                    static_step, static_slot, k0, nxt_q0, nxt_k0, diag, True))
            else:
                pl.when(here & jnp.logical_not(final))(functools.partial(
                    static_step, static_slot, k0, nxt_q0, nxt_k0, diag, True))
                pl.when(here & final)(functools.partial(
                    static_step, static_slot, k0, nxt_q0, nxt_k0, diag, False))

    def static_step(slot, k0, nxt_q0, nxt_k0, diag, issue_next):
        state = {(hd, qg): (m_ref[hd, :, qg * sub:(qg + 1) * sub],
                            acc_ref[hd, :, qg * sub:(qg + 1) * sub])
                 for hd in range(heads) for qg in range(groups)}

        def issue(n):
            if issue_next:
                ks, hd, qg = tiles[n]
                pend_ref[1 - slot, n] = scores(hd, qg, nxt_q0, nxt_k0 + ks * sub)

        for n in range(min(ATTN_ISSUE_AHEAD, len(tiles))):
            issue(n)
        for n, (ks, hd, qg) in enumerate(tiles):
            if n + ATTN_ISSUE_AHEAD < len(tiles):
                issue(n + ATTN_ISSUE_AHEAD)
            if diag and qg < ks:
                continue
            st = pend_ref[slot, n]
            if diag and qg == ks:
                key = lax.broadcasted_iota(jnp.int32, (sub, sub), 0)
                qry = lax.broadcasted_iota(jnp.int32, (sub, sub), 1)
                st = jnp.where(key <= qry, st, NEG_BIG)
            state[hd, qg] = update(hd, k0 + ks * sub, st, *state[hd, qg])
        for (hd, qg), (m, acc) in state.items():
            m_ref[hd, :, qg * sub:(qg + 1) * sub] = m
            acc_ref[hd, :, qg * sub:(qg + 1) * sub] = acc

    def q_block(j, slot):
        q0 = pl.multiple_of(j * blk, blk)
        m_ref[...] = jnp.full(m_ref.shape, NEG_BIG, jnp.float32)
        acc_ref[...] = jnp.zeros(acc_ref.shape, jnp.float32)

        def k_block(i, slot):
            k0 = pl.multiple_of(i * blk, blk)
            step(slot, k0, q0, k0 + blk, diag=False)
            return 1 - slot

        slot = lax.fori_loop(0, j, k_block, slot)
        nxt_q0 = pl.multiple_of(jnp.minimum(j + 1, n_q - 1) * blk, blk)
        step(slot, q0, nxt_q0, 0, diag=True, final=j == n_q - 1)

        for hd in range(heads):
            acc = acc_ref[hd]
            out = acc[:HEAD_DIM, :] / acc[HEAD_DIM:HEAD_DIM + 1, :]
            o_ref[0, hd * HEAD_DIM:(hd + 1) * HEAD_DIM, pl.ds(q0, blk)] = out.astype(o_ref.dtype)
        return 1 - slot

    for n, (ks, hd, qg) in enumerate(tiles):
        if qg >= ks:
            pend_ref[0, n] = scores(hd, qg, 0, ks * sub)
    lax.fori_loop(0, n_q, q_block, 0)


def _fox_attn(qp, kp, vtp):
    B, S, _ = qp.shape
    hp = ATTN_HEADS_PER_STEP
    return pl.pallas_call(
        _fox_attn_kernel,
        grid=(B, N_HEADS // hp),
        in_specs=[
            pl.BlockSpec((1, S, hp * HEAD_SLOT), lambda b, h: (b, 0, h)),
            pl.BlockSpec((1, S, hp * HEAD_SLOT), lambda b, h: (b, 0, h)),
            pl.BlockSpec((1, hp * VT_ROWS, S), lambda b, h: (b, h, 0)),
        ],
        out_specs=pl.BlockSpec((1, hp * HEAD_DIM, S), lambda b, h: (b, h, 0)),
        out_shape=jax.ShapeDtypeStruct((B, ATTN_WIDTH, S), jnp.bfloat16),
        scratch_shapes=[pltpu.VMEM((hp, 1, ATTN_BLOCK), jnp.float32),
                        pltpu.VMEM((hp, VT_ROWS, ATTN_BLOCK), jnp.float32),
                        pltpu.VMEM((2, hp * (ATTN_BLOCK // ATTN_UNIT) ** 2, ATTN_UNIT, ATTN_UNIT),
                                   jnp.float32)],
        compiler_params=pltpu.CompilerParams(
            dimension_semantics=("arbitrary", "arbitrary"), vmem_limit_bytes=VMEM_LIMIT),
        name="fox_attn",
    )(qp, kp, vtp)


def _mix_ffn_kernel(x_ref, at_ref, u_ref, uprev_ref, wpool_ref, pscale_ref, wout_ref, g2_ref,
                    wg_ref, wu_ref, wd_ref, gf_ref, y_ref, ext_ref, stage_ref, *, final_norm):
    tm = x_ref.shape[1]
    j = pl.program_id(1)

    n_groups = len(POOL_WINDOWS)
    out_chunk = D_MODEL // n_groups

    pad, top = POOL_PAD, POOL_PAD + POOL_HALO
    u = u_ref[0]
    ext_ref[:pad, :] = jnp.zeros((pad, POOL_WIDTH), jnp.float32)
    ext_ref[pad:top, :] = jnp.where(j > 0, uprev_ref[0], 0.0)
    ext_ref[top:, :] = u
    stage_ref[:, :pad, :] = jnp.zeros((stage_ref.shape[0], pad, POOL_GROUP), jnp.float32)

    first_rows = lax.broadcasted_iota(jnp.int32, (POOL_HALO, POOL_GROUP), 0) + 1
    mixed, x1_cols = [], []
    for gi, w in enumerate(POOL_WINDOWS):
        ocols = slice(gi * out_chunk, (gi + 1) * out_chunk)
        x1_cols.append(x_ref[0, :, ocols]
                       + lax.dot_general(at_ref[0], wout_ref[:ATTN_WIDTH, ocols], _TN,
                                         preferred_element_type=jnp.float32))
        cols = slice(gi * POOL_GROUP, (gi + 1) * POOL_GROUP)
        end = ext_ref.shape[0]
        wsum = ext_ref[pad:end, cols] + ext_ref[pad - 1:end - 1, cols]
        span = 2
        while span < w:
            level = stage_ref.at[(span.bit_length() - 2) % stage_ref.shape[0]]
            level[pad:end, :] = wsum
            wsum = wsum + level[pad - span:end - span, :]
            span *= 2
        wsum = wsum[POOL_HALO:, :]
        mean = wsum * (1.0 / w)
        exact_head = wsum[:POOL_HALO, :] / jnp.minimum(first_rows, w).astype(jnp.float32)
        head = jnp.where(j == 0, exact_head, mean[:POOL_HALO, :])
        mean = jnp.concatenate([head, mean[POOL_HALO:, :]], axis=0)
        pooled = (mean - u[:, cols]).astype(jnp.bfloat16)
        mg = jnp.dot(pooled, wpool_ref[gi], preferred_element_type=jnp.float32)
        mixed.append((mg * pscale_ref[:, cols]).astype(jnp.bfloat16))
    pool = jnp.concatenate(mixed, axis=1)
    x1 = jnp.concatenate(x1_cols, axis=1) + jnp.dot(pool, wout_ref[ATTN_WIDTH:, :],
                                                    preferred_element_type=jnp.float32)

    h2 = _rms(x1, g2_ref[...]).astype(jnp.bfloat16)
    ffn = jnp.zeros((tm, D_MODEL), jnp.float32)
    for c0 in range(0, D_FF, FF_CHUNK):
        gate = jnp.dot(h2, wg_ref[:, c0:c0 + FF_CHUNK], preferred_element_type=jnp.float32)
        up = jnp.dot(h2, wu_ref[:, c0:c0 + FF_CHUNK], preferred_element_type=jnp.float32)
        act = (gate / (1.0 + jnp.exp(-gate)) * up).astype(jnp.bfloat16)
        ffn = ffn + jnp.dot(act, wd_ref[c0:c0 + FF_CHUNK, :], preferred_element_type=jnp.float32)

    x2 = x1 + ffn
    y_ref[0] = _rms(x2, gf_ref[...]) if final_norm else x2


def _mix_ffn(x, attn_t, u, wpool, pscale, wout, g2, wg, wu, wd, gf, final_norm):
    B, S, D = x.shape
    tm = TOKEN_TILE
    halo_blocks = tm // POOL_HALO
    const = lambda shape: pl.BlockSpec(shape, lambda b, j: (0,) * len(shape),
                                       pipeline_mode=pl.Buffered(1))
    return pl.pallas_call(
        functools.partial(_mix_ffn_kernel, final_norm=final_norm),
        grid=(B, S // tm),
        in_specs=[
            pl.BlockSpec((1, tm, D), lambda b, j: (b, j, 0)),
            pl.BlockSpec((1, ATTN_WIDTH, tm), lambda b, j: (b, 0, j)),
            pl.BlockSpec((1, tm, POOL_WIDTH), lambda b, j: (b, j, 0)),
            pl.BlockSpec((1, POOL_HALO, POOL_WIDTH),
                         lambda b, j: (b, jnp.maximum(j * halo_blocks - 1, 0), 0)),
            const(wpool.shape), const(pscale.shape), const(wout.shape), const(g2.shape),
            const(wg.shape), const(wu.shape), const(wd.shape), const(gf.shape),
        ],
        out_specs=pl.BlockSpec((1, tm, D), lambda b, j: (b, j, 0)),
        out_shape=jax.ShapeDtypeStruct((B, S, D), jnp.float32),
        scratch_shapes=[pltpu.VMEM((POOL_PAD + POOL_HALO + tm, POOL_WIDTH), jnp.float32),
                        pltpu.VMEM((2, POOL_PAD + POOL_HALO + tm, POOL_GROUP), jnp.float32)],
        compiler_params=pltpu.CompilerParams(
            dimension_semantics=("arbitrary", "arbitrary"), vmem_limit_bytes=VMEM_LIMIT),
        name="mix_ffn",
    )(x, attn_t, u, u, wpool, pscale, wout, g2, wg, wu, wd, gf)


def kernel(x, norm1_g, w_in, b_forget, w_pool, pool_scale, w_out, norm2_g, w_gate, w_up, w_down, final_g):
    depth = w_in.shape[0]
    bf16 = jnp.bfloat16
    a0 = ATTN_WIDTH
    n_bias = N_BIAS_TERMS * N_HEADS
    lane = np.arange(LANES)
    mod3 = jnp.asarray(np.where(lane < n_bias, lane % N_BIAS_TERMS, N_BIAS_TERMS)[None, :], jnp.int32)
    for layer in range(depth):
        w = w_in[layer]
        wf = jnp.repeat(w[:, 3 * a0:3 * a0 + N_HEADS], N_BIAS_TERMS, axis=1)
        w_all = jnp.concatenate([w[:, :3 * a0], w[:, 3 * a0 + N_HEADS:], wf,
                                 jnp.zeros((D_MODEL, LANES - n_bias), w.dtype)], axis=1).astype(bf16)
        bf = jnp.concatenate([jnp.repeat(b_forget[layer], N_BIAS_TERMS),
                              jnp.zeros((LANES - n_bias,), b_forget.dtype)]).astype(jnp.float32)[None, :]
        (qp, kp, vtp, u), (wg, wu, wd, wo, wp) = _in_proj(
            x, norm1_g[layer][None, :], w_all, bf, mod3,
            (w_gate[layer], w_up[layer], w_down[layer], w_out[layer],
             w_pool[layer].reshape(POOL_WIDTH, POOL_GROUP)))
        attn_t = _fox_attn(qp, kp, vtp)
        x = _mix_ffn(x, attn_t, u, wp.reshape(w_pool.shape[1:]), pool_scale[layer][None, :],
                     wo, norm2_g[layer][None, :], wg, wu, wd,
                     final_g[None, :], final_norm=layer == depth - 1)
    return x
```

```python
import functools
import math

import jax
import jax.numpy as jnp
import numpy as np
from jax import lax
from jax.experimental import pallas as pl
from jax.experimental.pallas import tpu as pltpu

D_MODEL = 1024
ATTN_WIDTH = 512
N_HEADS = 8
HEAD_DIM = 64
POOL_WIDTH = 512
POOL_WINDOWS = (2, 4, 8, 16)
POOL_GROUP = 128
D_FF = 2816
EPS = 1e-6

LANES = 128
BF16_SUBLANES = 16
HEAD_SLOT = 128
N_BIAS_TERMS = 3
VT_ROWS = HEAD_DIM + BF16_SUBLANES
POOL_HALO = 16
POOL_PAD = 8
LOG2E = math.log2(math.e)
NEG_BIG = -1e30

TOKEN_TILE = 1024
IN_PROJ_TILE = 1024
ATTN_BLOCK = 1024
ATTN_UNIT = 256
ATTN_HEADS_PER_STEP = 2
ATTN_ISSUE_AHEAD = 2
FF_CHUNK = 256
VMEM_LIMIT = 56 * 1024 * 1024

_NT = (((1,), (1,)), ((), ()))
_TN = (((0,), (0,)), ((), ()))


def _rms(x, g):
    ms = jnp.mean(x * x, axis=-1, keepdims=True)
    return x * lax.rsqrt(ms + EPS) * g


def _split_bf16(x):
    hi = x.astype(jnp.bfloat16)
    r1 = x - hi.astype(jnp.float32)
    mid = r1.astype(jnp.bfloat16)
    lo = (r1 - mid.astype(jnp.float32)).astype(jnp.bfloat16)
    return hi, mid, lo


def _in_proj_kernel(x_ref, g_ref, w_ref, bf_ref, mod3_ref, *rest, n_cast):
    cast_in = rest[:n_cast]
    qp_ref, kp_ref, vt_ref, u_ref = rest[n_cast:n_cast + 4]
    cast_out = rest[n_cast + 4:2 * n_cast + 4]
    carry_ref, scan_ref = rest[2 * n_cast + 4:]
    tm = x_ref.shape[1]
    pad = scan_ref.shape[1] - tm

    @pl.when(pl.program_id(1) == 0)
    def _():
        carry_ref[...] = jnp.zeros_like(carry_ref)

    h = _rms(x_ref[0], g_ref[...]).astype(jnp.bfloat16)
    qk_end, v_end = 2 * ATTN_WIDTH, 3 * ATTN_WIDTH
    uf = jnp.dot(h, w_ref[:, v_end:], preferred_element_type=jnp.float32)
    u_ref[0] = uf[:, :POOL_WIDTH]
    f = uf[:, POOL_WIDTH:] + bf_ref[...]
    vt = jnp.dot(h, w_ref[:, qk_end:v_end], preferred_element_type=jnp.float32).T
    qk = jnp.dot(h, w_ref[:, :qk_end], preferred_element_type=jnp.float32)

    for src, dst in zip(cast_in, cast_out):
        dst[...] = src[...].astype(dst.dtype)

    lf = jnp.minimum(f, 0.0) - jnp.log(1.0 + jnp.exp(-jnp.abs(f)))
    scan_ref[:, :pad, :] = jnp.zeros((scan_ref.shape[0], pad, LANES), jnp.float32)
    shift, level = 1, 0
    while shift < tm:
        scan_ref[level, pad:, :] = lf
        lf = lf + scan_ref[level, pad - shift:pad - shift + tm, :]
        shift, level = 2 * shift, 1 - level
    c = lf + carry_ref[...]
    carry_ref[...] = c[tm - 1:tm, :]

    hi, mid, lo = _split_bf16(c * (-LOG2E))
    m3 = mod3_ref[...]
    e3 = jnp.where(m3 == 0, hi.astype(jnp.float32),
                   jnp.where(m3 == 1, mid.astype(jnp.float32), lo.astype(jnp.float32)))

    lane = lax.broadcasted_iota(jnp.int32, (tm, LANES), 1)
    low_half = lane < HEAD_DIM
    ones_hi = jnp.where((lane >= HEAD_DIM) & (lane < HEAD_DIM + N_BIAS_TERMS), 1.0, 0.0)
    ones_lo = jnp.where(lane < N_BIAS_TERMS, 1.0, 0.0)
    q_scale = HEAD_DIM ** -0.5 * LOG2E
    for hd in range(N_HEADS):
        pair = hd // 2
        qpair = qk[:, pair * LANES:(pair + 1) * LANES] * q_scale
        kpair = qk[:, ATTN_WIDTH + pair * LANES:ATTN_WIDTH + (pair + 1) * LANES]
        if hd % 2 == 0:
            bias = pltpu.roll(e3, HEAD_DIM - N_BIAS_TERMS * hd, 1)
            qh = jnp.where(low_half, qpair, ones_hi)
            kh = jnp.where(low_half, kpair, bias)
        else:
            bias = pltpu.roll(e3, LANES - N_BIAS_TERMS * hd, 1)
            qh = jnp.where(low_half, ones_lo, qpair)
            kh = jnp.where(low_half, bias, kpair)
        qp_ref[0, :, hd * HEAD_SLOT:(hd + 1) * HEAD_SLOT] = qh.astype(jnp.bfloat16)
        kp_ref[0, :, hd * HEAD_SLOT:(hd + 1) * HEAD_SLOT] = kh.astype(jnp.bfloat16)
        vt_ref[0, hd * VT_ROWS:hd * VT_ROWS + HEAD_DIM, :] = (
            vt[hd * HEAD_DIM:(hd + 1) * HEAD_DIM, :].astype(jnp.bfloat16))
        vt_ref[0, hd * VT_ROWS + HEAD_DIM:(hd + 1) * VT_ROWS, :] = jnp.ones(
            (VT_ROWS - HEAD_DIM, tm), jnp.bfloat16)


def _in_proj(x, g, w_all, bf, mod3, ffn_weights):
    B, S, D = x.shape
    tm = IN_PROJ_TILE
    nj = S // tm
    n_steps = B * nj
    const = lambda shape: pl.BlockSpec(shape, lambda b, j: (0,) * len(shape))

    def slab_spec(w):
        rows = w.shape[0]
        n_slabs = max(k for k in range(1, n_steps + 1)
                      if rows % k == 0 and (rows // k) % BF16_SUBLANES == 0)
        return pl.BlockSpec((rows // n_slabs, w.shape[1]),
                            lambda b, j: (jnp.minimum(b * nj + j, n_slabs - 1), 0))

    outs = pl.pallas_call(
        functools.partial(_in_proj_kernel, n_cast=len(ffn_weights)),
        grid=(B, nj),
        in_specs=[
            pl.BlockSpec((1, tm, D), lambda b, j: (b, j, 0)),
            const(g.shape), const(w_all.shape), const(bf.shape), const(mod3.shape),
        ] + [slab_spec(w) for w in ffn_weights],
        out_specs=[
            pl.BlockSpec((1, tm, N_HEADS * HEAD_SLOT), lambda b, j: (b, j, 0)),
            pl.BlockSpec((1, tm, N_HEADS * HEAD_SLOT), lambda b, j: (b, j, 0)),
            pl.BlockSpec((1, N_HEADS * VT_ROWS, tm), lambda b, j: (b, 0, j)),
            pl.BlockSpec((1, tm, POOL_WIDTH), lambda b, j: (b, j, 0)),
        ] + [slab_spec(w) for w in ffn_weights],
        out_shape=[
            jax.ShapeDtypeStruct((B, S, N_HEADS * HEAD_SLOT), jnp.bfloat16),
            jax.ShapeDtypeStruct((B, S, N_HEADS * HEAD_SLOT), jnp.bfloat16),
            jax.ShapeDtypeStruct((B, N_HEADS * VT_ROWS, S), jnp.bfloat16),
            jax.ShapeDtypeStruct((B, S, POOL_WIDTH), jnp.float32),
        ] + [jax.ShapeDtypeStruct(w.shape, jnp.bfloat16) for w in ffn_weights],
        scratch_shapes=[pltpu.VMEM((1, LANES), jnp.float32),
                        pltpu.VMEM((2, tm // 2 + tm, LANES), jnp.float32)],
        compiler_params=pltpu.CompilerParams(
            dimension_semantics=("arbitrary", "arbitrary"), vmem_limit_bytes=VMEM_LIMIT),
        name="in_proj",
    )(x, g, w_all, bf, mod3, *ffn_weights)
    return outs[:4], outs[4:]


def _fox_attn_kernel(q_ref, k_ref, vt_ref, o_ref, m_ref, acc_ref, pend_ref):
    S = q_ref.shape[1]
    blk, sub = ATTN_BLOCK, ATTN_UNIT
    heads = q_ref.shape[2] // HEAD_SLOT
    groups = blk // sub
    n_q = S // blk
    tiles = [(ks, hd, qg) for ks in range(groups) for hd in range(heads) for qg in range(groups)]

    def scores(hd, qg, q0, k0):
        lanes = slice(hd * HEAD_SLOT, (hd + 1) * HEAD_SLOT)
        qt = q_ref[0, pl.ds(q0 + qg * sub, sub), lanes]
        kt = k_ref[0, pl.ds(k0, sub), lanes]
        return lax.dot_general(kt, qt, _NT, preferred_element_type=jnp.float32)

    def update(hd, k0, st, m_old, acc_old):
        m_new = jnp.maximum(m_old, jnp.max(st, axis=0, keepdims=True))
        p = jnp.exp2(st - m_new).astype(jnp.bfloat16)
        alpha = jnp.exp2(m_old - m_new)
        vt = vt_ref[0, hd * VT_ROWS:(hd + 1) * VT_ROWS, pl.ds(k0, sub)]
        return m_new, alpha * acc_old + jnp.dot(vt, p, preferred_element_type=jnp.float32)

    def step(slot, k0, nxt_q0, nxt_k0, diag, final=None):
        for static_slot in range(2):
            here = slot == static_slot
            if final is None:
                pl.when(here)(functools.partial(
                    static_step, static_slot, k0, nxt_q0, nxt_k0, diag, True))
            else:
                pl.when(here & jnp.logical_not(final))(functools.partial(
                    static_step, static_slot, k0, nxt_q0, nxt_k0, diag, True))
                pl.when(here & final)(functools.partial(
                    static_step, static_slot, k0, nxt_q0, nxt_k0, diag, False))

    def static_step(slot, k0, nxt_q0, nxt_k0, diag, issue_next):
        state = {(hd, qg): (m_ref[hd, :, qg * sub:(qg + 1) * sub],
                            acc_ref[hd, :, qg * sub:(qg + 1) * sub])
                 for hd in range(heads) for qg in range(groups)}

        def issue(n):
            if issue_next:
                ks, hd, qg = tiles[n]
                pend_ref[1 - slot, n] = scores(hd, qg, nxt_q0, nxt_k0 + ks * sub)

        for n in range(min(ATTN_ISSUE_AHEAD, len(tiles))):
            issue(n)
        for n, (ks, hd, qg) in enumerate(tiles):
            if n + ATTN_ISSUE_AHEAD < len(tiles):
                issue(n + ATTN_ISSUE_AHEAD)
            if diag and qg < ks:
                continue
            st = pend_ref[slot, n]
            if diag and qg == ks:
                key = lax.broadcasted_iota(jnp.int32, (sub, sub), 0)
                qry = lax.broadcasted_iota(jnp.int32, (sub, sub), 1)
                st = jnp.where(key <= qry, st, NEG_BIG)
            state[hd, qg] = update(hd, k0 + ks * sub, st, *state[hd, qg])
        for (hd, qg), (m, acc) in state.items():
            m_ref[hd, :, qg * sub:(qg + 1) * sub] = m
            acc_ref[hd, :, qg * sub:(qg + 1) * sub] = acc

    def q_block(j, slot):
        q0 = pl.multiple_of(j * blk, blk)
        m_ref[...] = jnp.full(m_ref.shape, NEG_BIG, jnp.float32)
        acc_ref[...] = jnp.zeros(acc_ref.shape, jnp.float32)

        def k_block(i, slot):
            k0 = pl.multiple_of(i * blk, blk)
            step(slot, k0, q0, k0 + blk, diag=False)
            return 1 - slot

        slot = lax.fori_loop(0, j, k_block, slot)
        nxt_q0 = pl.multiple_of(jnp.minimum(j + 1, n_q - 1) * blk, blk)
        step(slot, q0, nxt_q0, 0, diag=True, final=j == n_q - 1)

        for hd in range(heads):
            acc = acc_ref[hd]
            out = acc[:HEAD_DIM, :] / acc[HEAD_DIM:HEAD_DIM + 1, :]
            o_ref[0, hd * HEAD_DIM:(hd + 1) * HEAD_DIM, pl.ds(q0, blk)] = out.astype(o_ref.dtype)
        return 1 - slot

    for n, (ks, hd, qg) in enumerate(tiles):
        if qg >= ks:
            pend_ref[0, n] = scores(hd, qg, 0, ks * sub)
    lax.fori_loop(0, n_q, q_block, 0)


def _fox_attn(qp, kp, vtp):
    B, S, _ = qp.shape
    hp = ATTN_HEADS_PER_STEP
    return pl.pallas_call(
        _fox_attn_kernel,
        grid=(B, N_HEADS // hp),
        in_specs=[
            pl.BlockSpec((1, S, hp * HEAD_SLOT), lambda b, h: (b, 0, h)),
            pl.BlockSpec((1, S, hp * HEAD_SLOT), lambda b, h: (b, 0, h)),
            pl.BlockSpec((1, hp * VT_ROWS, S), lambda b, h: (b, h, 0)),
        ],
        out_specs=pl.BlockSpec((1, hp * HEAD_DIM, S), lambda b, h: (b, h, 0)),
        out_shape=jax.ShapeDtypeStruct((B, ATTN_WIDTH, S), jnp.bfloat16),
        scratch_shapes=[pltpu.VMEM((hp, 1, ATTN_BLOCK), jnp.float32),
                        pltpu.VMEM((hp, VT_ROWS, ATTN_BLOCK), jnp.float32),
                        pltpu.VMEM((2, hp * (ATTN_BLOCK // ATTN_UNIT) ** 2, ATTN_UNIT, ATTN_UNIT),
                                   jnp.float32)],
        compiler_params=pltpu.CompilerParams(
            dimension_semantics=("arbitrary", "arbitrary"), vmem_limit_bytes=VMEM_LIMIT),
        name="fox_attn",
    )(qp, kp, vtp)


def _mix_ffn_kernel(x_ref, at_ref, u_ref, uprev_ref, wpool_ref, pscale_ref, wout_ref, g2_ref,
                    wg_ref, wu_ref, wd_ref, gf_ref, y_ref, ext_ref, stage_ref, *, final_norm):
    tm = x_ref.shape[1]
    j = pl.program_id(1)

    n_groups = len(POOL_WINDOWS)
    out_chunk = D_MODEL // n_groups

    pad, top = POOL_PAD, POOL_PAD + POOL_HALO
    u = u_ref[0]
    ext_ref[:pad, :] = jnp.zeros((pad, POOL_WIDTH), jnp.float32)
    ext_ref[pad:top, :] = jnp.where(j > 0, uprev_ref[0], 0.0)
    ext_ref[top:, :] = u
    stage_ref[:, :pad, :] = jnp.zeros((stage_ref.shape[0], pad, POOL_GROUP), jnp.float32)

    first_rows = lax.broadcasted_iota(jnp.int32, (POOL_HALO, POOL_GROUP), 0) + 1
    mixed, x1_cols = [], []
    for gi, w in enumerate(POOL_WINDOWS):
        ocols = slice(gi * out_chunk, (gi + 1) * out_chunk)
        x1_cols.append(x_ref[0, :, ocols]
                       + lax.dot_general(at_ref[0], wout_ref[:ATTN_WIDTH, ocols], _TN,
                                         preferred_element_type=jnp.float32))
        cols = slice(gi * POOL_GROUP, (gi + 1) * POOL_GROUP)
        end = ext_ref.shape[0]
        wsum = ext_ref[pad:end, cols] + ext_ref[pad - 1:end - 1, cols]
        span = 2
        while span < w:
            level = stage_ref.at[(span.bit_length() - 2) % stage_ref.shape[0]]
            level[pad:end, :] = wsum
            wsum = wsum + level[pad - span:end - span, :]
            span *= 2
        wsum = wsum[POOL_HALO:, :]
        mean = wsum * (1.0 / w)
        exact_head = wsum[:POOL_HALO, :] / jnp.minimum(first_rows, w).astype(jnp.float32)
        head = jnp.where(j == 0, exact_head, mean[:POOL_HALO, :])
        mean = jnp.concatenate([head, mean[POOL_HALO:, :]], axis=0)
        pooled = (mean - u[:, cols]).astype(jnp.bfloat16)
        mg = jnp.dot(pooled, wpool_ref[gi], preferred_element_type=jnp.float32)
        mixed.append((mg * pscale_ref[:, cols]).astype(jnp.bfloat16))
    pool = jnp.concatenate(mixed, axis=1)
    x1 = jnp.concatenate(x1_cols, axis=1) + jnp.dot(pool, wout_ref[ATTN_WIDTH:, :],
                                                    preferred_element_type=jnp.float32)

    h2 = _rms(x1, g2_ref[...]).astype(jnp.bfloat16)
    ffn = jnp.zeros((tm, D_MODEL), jnp.float32)
    for c0 in range(0, D_FF, FF_CHUNK):
        gate = jnp.dot(h2, wg_ref[:, c0:c0 + FF_CHUNK], preferred_element_type=jnp.float32)
        up = jnp.dot(h2, wu_ref[:, c0:c0 + FF_CHUNK], preferred_element_type=jnp.float32)
        act = (gate / (1.0 + jnp.exp(-gate)) * up).astype(jnp.bfloat16)
        ffn = ffn + jnp.dot(act, wd_ref[c0:c0 + FF_CHUNK, :], preferred_element_type=jnp.float32)

    x2 = x1 + ffn
    y_ref[0] = _rms(x2, gf_ref[...]) if final_norm else x2


def _mix_ffn(x, attn_t, u, wpool, pscale, wout, g2, wg, wu, wd, gf, final_norm):
    B, S, D = x.shape
    tm = TOKEN_TILE
    halo_blocks = tm // POOL_HALO
    const = lambda shape: pl.BlockSpec(shape, lambda b, j: (0,) * len(shape),
                                       pipeline_mode=pl.Buffered(1))
    return pl.pallas_call(
        functools.partial(_mix_ffn_kernel, final_norm=final_norm),
        grid=(B, S // tm),
        in_specs=[
            pl.BlockSpec((1, tm, D), lambda b, j: (b, j, 0)),
            pl.BlockSpec((1, ATTN_WIDTH, tm), lambda b, j: (b, 0, j)),
            pl.BlockSpec((1, tm, POOL_WIDTH), lambda b, j: (b, j, 0)),
            pl.BlockSpec((1, POOL_HALO, POOL_WIDTH),
                         lambda b, j: (b, jnp.maximum(j * halo_blocks - 1, 0), 0)),
            const(wpool.shape), const(pscale.shape), const(wout.shape), const(g2.shape),
            const(wg.shape), const(wu.shape), const(wd.shape), const(gf.shape),
        ],
        out_specs=pl.BlockSpec((1, tm, D), lambda b, j: (b, j, 0)),
        out_shape=jax.ShapeDtypeStruct((B, S, D), jnp.float32),
        scratch_shapes=[pltpu.VMEM((POOL_PAD + POOL_HALO + tm, POOL_WIDTH), jnp.float32),
                        pltpu.VMEM((2, POOL_PAD + POOL_HALO + tm, POOL_GROUP), jnp.float32)],
        compiler_params=pltpu.CompilerParams(
            dimension_semantics=("arbitrary", "arbitrary"), vmem_limit_bytes=VMEM_LIMIT),
        name="mix_ffn",
    )(x, attn_t, u, u, wpool, pscale, wout, g2, wg, wu, wd, gf)


def kernel(x, norm1_g, w_in, b_forget, w_pool, pool_scale, w_out, norm2_g, w_gate, w_up, w_down, final_g):
    depth = w_in.shape[0]
    bf16 = jnp.bfloat16
    a0 = ATTN_WIDTH
    n_bias = N_BIAS_TERMS * N_HEADS
    lane = np.arange(LANES)
    mod3 = jnp.asarray(np.where(lane < n_bias, lane % N_BIAS_TERMS, N_BIAS_TERMS)[None, :], jnp.int32)
    for layer in range(depth):
        w = w_in[layer]
        wf = jnp.repeat(w[:, 3 * a0:3 * a0 + N_HEADS], N_BIAS_TERMS, axis=1)
        w_all = jnp.concatenate([w[:, :3 * a0], w[:, 3 * a0 + N_HEADS:], wf,
                                 jnp.zeros((D_MODEL, LANES - n_bias), w.dtype)], axis=1).astype(bf16)
        bf = jnp.concatenate([jnp.repeat(b_forget[layer], N_BIAS_TERMS),
                              jnp.zeros((LANES - n_bias,), b_forget.dtype)]).astype(jnp.float32)[None, :]
        (qp, kp, vtp, u), (wg, wu, wd, wo, wp) = _in_proj(
            x, norm1_g[layer][None, :], w_all, bf, mod3,
            (w_gate[layer], w_up[layer], w_down[layer], w_out[layer],
             w_pool[layer].reshape(POOL_WIDTH, POOL_GROUP)))
        attn_t = _fox_attn(qp, kp, vtp)
        x = _mix_ffn(x, attn_t, u, wp.reshape(w_pool.shape[1:]), pool_scale[layer][None, :],
                     wo, norm2_g[layer][None, :], wg, wu, wd,
                     final_g[None, :], final_norm=layer == depth - 1)
    return x
```

```python
import functools
import math

import jax
import jax.numpy as jnp
import numpy as np
from jax import lax
from jax.experimental import pallas as pl
from jax.experimental.pallas import tpu as pltpu

D_MODEL = 1024
ATTN_WIDTH = 512
N_HEADS = 8
HEAD_DIM = 64
POOL_WIDTH = 512
POOL_WINDOWS = (2, 4, 8, 16)
POOL_GROUP = 128
D_FF = 2816
EPS = 1e-6

LANES = 128
BF16_SUBLANES = 16
HEAD_SLOT = 128
N_BIAS_TERMS = 3
VT_ROWS = HEAD_DIM + BF16_SUBLANES
POOL_HALO = 16
POOL_PAD = 8
LOG2E = math.log2(math.e)
NEG_BIG = -1e30

TOKEN_TILE = 1024
IN_PROJ_TILE = 1024
ATTN_BLOCK = 512
ATTN_UNIT = 256
ATTN_HEADS_PER_STEP = 4
ATTN_ISSUE_AHEAD = 3
FF_CHUNK = 256
VMEM_LIMIT = 56 * 1024 * 1024

_NT = (((1,), (1,)), ((), ()))
_TN = (((0,), (0,)), ((), ()))


def _rms(x, g):
    ms = jnp.mean(x * x, axis=-1, keepdims=True)
    return x * lax.rsqrt(ms + EPS) * g


def _split_bf16(x):
    hi = x.astype(jnp.bfloat16)
    r1 = x - hi.astype(jnp.float32)
    mid = r1.astype(jnp.bfloat16)
    lo = (r1 - mid.astype(jnp.float32)).astype(jnp.bfloat16)
    return hi, mid, lo


def _in_proj_kernel(x_ref, g_ref, w_ref, bf_ref, mod3_ref, *rest, n_cast):
    cast_in = rest[:n_cast]
    qp_ref, kp_ref, vt_ref, u_ref = rest[n_cast:n_cast + 4]
    cast_out = rest[n_cast + 4:2 * n_cast + 4]
    carry_ref, scan_ref = rest[2 * n_cast + 4:]
    tm = x_ref.shape[1]
    pad = scan_ref.shape[1] - tm

    @pl.when(pl.program_id(1) == 0)
    def _():
        carry_ref[...] = jnp.zeros_like(carry_ref)

    h = _rms(x_ref[0], g_ref[...]).astype(jnp.bfloat16)
    qk_end, v_end = 2 * ATTN_WIDTH, 3 * ATTN_WIDTH
    uf = jnp.dot(h, w_ref[:, v_end:], preferred_element_type=jnp.float32)
    u_ref[0] = uf[:, :POOL_WIDTH]
    f = uf[:, POOL_WIDTH:] + bf_ref[...]
    vt = jnp.dot(h, w_ref[:, qk_end:v_end], preferred_element_type=jnp.float32).T
    qk = jnp.dot(h, w_ref[:, :qk_end], preferred_element_type=jnp.float32)

    for src, dst in zip(cast_in, cast_out):
        dst[...] = src[...].astype(dst.dtype)

    lf = jnp.minimum(f, 0.0) - jnp.log(1.0 + jnp.exp(-jnp.abs(f)))
    scan_ref[:, :pad, :] = jnp.zeros((scan_ref.shape[0], pad, LANES), jnp.float32)
    shift, level = 1, 0
    while shift < tm:
        scan_ref[level, pad:, :] = lf
        lf = lf + scan_ref[level, pad - shift:pad - shift + tm, :]
        shift, level = 2 * shift, 1 - level
    c = lf + carry_ref[...]
    carry_ref[...] = c[tm - 1:tm, :]

    hi, mid, lo = _split_bf16(c * (-LOG2E))
    m3 = mod3_ref[...]
    e3 = jnp.where(m3 == 0, hi.astype(jnp.float32),
                   jnp.where(m3 == 1, mid.astype(jnp.float32), lo.astype(jnp.float32)))

    lane = lax.broadcasted_iota(jnp.int32, (tm, LANES), 1)
    low_half = lane < HEAD_DIM
    ones_hi = jnp.where((lane >= HEAD_DIM) & (lane < HEAD_DIM + N_BIAS_TERMS), 1.0, 0.0)
    ones_lo = jnp.where(lane < N_BIAS_TERMS, 1.0, 0.0)
    q_scale = HEAD_DIM ** -0.5 * LOG2E
    for hd in range(N_HEADS):
        pair = hd // 2
        qpair = qk[:, pair * LANES:(pair + 1) * LANES] * q_scale
        kpair = qk[:, ATTN_WIDTH + pair * LANES:ATTN_WIDTH + (pair + 1) * LANES]
        if hd % 2 == 0:
            bias = pltpu.roll(e3, HEAD_DIM - N_BIAS_TERMS * hd, 1)
            qh = jnp.where(low_half, qpair, ones_hi)
            kh = jnp.where(low_half, kpair, bias)
        else:
            bias = pltpu.roll(e3, LANES - N_BIAS_TERMS * hd, 1)
            qh = jnp.where(low_half, ones_lo, qpair)
            kh = jnp.where(low_half, bias, kpair)
        qp_ref[0, :, hd * HEAD_SLOT:(hd + 1) * HEAD_SLOT] = qh.astype(jnp.bfloat16)
        kp_ref[0, :, hd * HEAD_SLOT:(hd + 1) * HEAD_SLOT] = kh.astype(jnp.bfloat16)
        vt_ref[0, hd * VT_ROWS:hd * VT_ROWS + HEAD_DIM, :] = (
            vt[hd * HEAD_DIM:(hd + 1) * HEAD_DIM, :].astype(jnp.bfloat16))
        vt_ref[0, hd * VT_ROWS + HEAD_DIM:(hd + 1) * VT_ROWS, :] = jnp.ones(
            (VT_ROWS - HEAD_DIM, tm), jnp.bfloat16)


def _in_proj(x, g, w_all, bf, mod3, ffn_weights):
    B, S, D = x.shape
    tm = IN_PROJ_TILE
    nj = S // tm
    n_steps = B * nj
    const = lambda shape: pl.BlockSpec(shape, lambda b, j: (0,) * len(shape))

    def slab_spec(w):
        rows = w.shape[0]
        n_slabs = max(k for k in range(1, n_steps + 1)
                      if rows % k == 0 and (rows // k) % BF16_SUBLANES == 0)
        return pl.BlockSpec((rows // n_slabs, w.shape[1]),
                            lambda b, j: (jnp.minimum(b * nj + j, n_slabs - 1), 0))

    outs = pl.pallas_call(
        functools.partial(_in_proj_kernel, n_cast=len(ffn_weights)),
        grid=(B, nj),
        in_specs=[
            pl.BlockSpec((1, tm, D), lambda b, j: (b, j, 0)),
            const(g.shape), const(w_all.shape), const(bf.shape), const(mod3.shape),
        ] + [slab_spec(w) for w in ffn_weights],
        out_specs=[
            pl.BlockSpec((1, tm, N_HEADS * HEAD_SLOT), lambda b, j: (b, j, 0)),
            pl.BlockSpec((1, tm, N_HEADS * HEAD_SLOT), lambda b, j: (b, j, 0)),
            pl.BlockSpec((1, N_HEADS * VT_ROWS, tm), lambda b, j: (b, 0, j)),
            pl.BlockSpec((1, tm, POOL_WIDTH), lambda b, j: (b, j, 0)),
        ] + [slab_spec(w) for w in ffn_weights],
        out_shape=[
            jax.ShapeDtypeStruct((B, S, N_HEADS * HEAD_SLOT), jnp.bfloat16),
            jax.ShapeDtypeStruct((B, S, N_HEADS * HEAD_SLOT), jnp.bfloat16),
            jax.ShapeDtypeStruct((B, N_HEADS * VT_ROWS, S), jnp.bfloat16),
            jax.ShapeDtypeStruct((B, S, POOL_WIDTH), jnp.float32),
        ] + [jax.ShapeDtypeStruct(w.shape, jnp.bfloat16) for w in ffn_weights],
        scratch_shapes=[pltpu.VMEM((1, LANES), jnp.float32),
                        pltpu.VMEM((2, tm // 2 + tm, LANES), jnp.float32)],
        compiler_params=pltpu.CompilerParams(
            dimension_semantics=("arbitrary", "arbitrary"), vmem_limit_bytes=VMEM_LIMIT),
        name="in_proj",
    )(x, g, w_all, bf, mod3, *ffn_weights)
    return outs[:4], outs[4:]


def _fox_attn_kernel(q_ref, k_ref, vt_ref, o_ref, m_ref, acc_ref, pend_ref):
    S = q_ref.shape[1]
    blk, sub = ATTN_BLOCK, ATTN_UNIT
    heads = q_ref.shape[2] // HEAD_SLOT
    groups = blk // sub
    n_q = S // blk
    tiles = [(ks, hd, qg) for ks in range(groups) for hd in range(heads) for qg in range(groups)]

    def scores(hd, qg, q0, k0):
        lanes = slice(hd * HEAD_SLOT, (hd + 1) * HEAD_SLOT)
        qt = q_ref[0, pl.ds(q0 + qg * sub, sub), lanes]
        kt = k_ref[0, pl.ds(k0, sub), lanes]
        return lax.dot_general(kt, qt, _NT, preferred_element_type=jnp.float32)

    def update(hd, k0, st, m_old, acc_old):
        m_new = jnp.maximum(m_old, jnp.max(st, axis=0, keepdims=True))
        p = jnp.exp2(st - m_new).astype(jnp.bfloat16)
        alpha = jnp.exp2(m_old - m_new)
        vt = vt_ref[0, hd * VT_ROWS:(hd + 1) * VT_ROWS, pl.ds(k0, sub)]
        return m_new, alpha * acc_old + jnp.dot(vt, p, preferred_element_type=jnp.float32)

    def step(slot, k0, nxt_q0, nxt_k0, diag, final=None):
        for static_slot in range(2):
            here = slot == static_slot
            if final is None:
                pl.when(here)(functools.partial(
                    static_step, static_slot, k0, nxt_q0, nxt_k0, diag, True))
            else:
                pl.when(here & jnp.logical_not(final))(functools.partial(
                    static_step, static_slot, k0, nxt_q0, nxt_k0, diag, True))
                pl.when(here & final)(functools.partial(
                    static_step, static_slot, k0, nxt_q0, nxt_k0, diag, False))

    def static_step(slot, k0, nxt_q0, nxt_k0, diag, issue_next):
        state = {(hd, qg): (m_ref[hd, :, qg * sub:(qg + 1) * sub],
                            acc_ref[hd, :, qg * sub:(qg + 1) * sub])
                 for hd in range(heads) for qg in range(groups)}

        def issue(n):
            if issue_next:
                ks, hd, qg = tiles[n]
                pend_ref[1 - slot, n] = scores(hd, qg, nxt_q0, nxt_k0 + ks * sub)

        for n in range(min(ATTN_ISSUE_AHEAD, len(tiles))):
            issue(n)
        for n, (ks, hd, qg) in enumerate(tiles):
            if n + ATTN_ISSUE_AHEAD < len(tiles):
                issue(n + ATTN_ISSUE_AHEAD)
            if diag and qg < ks:
                continue
            st = pend_ref[slot, n]
            if diag and qg == ks:
                key = lax.broadcasted_iota(jnp.int32, (sub, sub), 0)
                qry = lax.broadcasted_iota(jnp.int32, (sub, sub), 1)
                st = jnp.where(key <= qry, st, NEG_BIG)
            state[hd, qg] = update(hd, k0 + ks * sub, st, *state[hd, qg])
        for (hd, qg), (m, acc) in state.items():
            m_ref[hd, :, qg * sub:(qg + 1) * sub] = m
            acc_ref[hd, :, qg * sub:(qg + 1) * sub] = acc

    def q_block(j, slot):
        q0 = pl.multiple_of(j * blk, blk)
        m_ref[...] = jnp.full(m_ref.shape, NEG_BIG, jnp.float32)
        acc_ref[...] = jnp.zeros(acc_ref.shape, jnp.float32)

        def k_block(i, slot):
            k0 = pl.multiple_of(i * blk, blk)
            step(slot, k0, q0, k0 + blk, diag=False)
            return 1 - slot

        slot = lax.fori_loop(0, j, k_block, slot)
        nxt_q0 = pl.multiple_of(jnp.minimum(j + 1, n_q - 1) * blk, blk)
        step(slot, q0, nxt_q0, 0, diag=True, final=j == n_q - 1)

        for hd in range(heads):
            acc = acc_ref[hd]
            out = acc[:HEAD_DIM, :] / acc[HEAD_DIM:HEAD_DIM + 1, :]
            o_ref[0, hd * HEAD_DIM:(hd + 1) * HEAD_DIM, pl.ds(q0, blk)] = out.astype(o_ref.dtype)
        return 1 - slot

    for n, (ks, hd, qg) in enumerate(tiles):
        if qg >= ks:
            pend_ref[0, n] = scores(hd, qg, 0, ks * sub)
    lax.fori_loop(0, n_q, q_block, 0)


def _fox_attn(qp, kp, vtp):
    B, S, _ = qp.shape
    hp = ATTN_HEADS_PER_STEP
    return pl.pallas_call(
        _fox_attn_kernel,
        grid=(B, N_HEADS // hp),
        in_specs=[
            pl.BlockSpec((1, S, hp * HEAD_SLOT), lambda b, h: (b, 0, h)),
            pl.BlockSpec((1, S, hp * HEAD_SLOT), lambda b, h: (b, 0, h)),
            pl.BlockSpec((1, hp * VT_ROWS, S), lambda b, h: (b, h, 0)),
        ],
        out_specs=pl.BlockSpec((1, hp * HEAD_DIM, S), lambda b, h: (b, h, 0)),
        out_shape=jax.ShapeDtypeStruct((B, ATTN_WIDTH, S), jnp.bfloat16),
        scratch_shapes=[pltpu.VMEM((hp, 1, ATTN_BLOCK), jnp.float32),
                        pltpu.VMEM((hp, VT_ROWS, ATTN_BLOCK), jnp.float32),
                        pltpu.VMEM((2, hp * (ATTN_BLOCK // ATTN_UNIT) ** 2, ATTN_UNIT, ATTN_UNIT),
                                   jnp.float32)],
        compiler_params=pltpu.CompilerParams(
            dimension_semantics=("arbitrary", "arbitrary"), vmem_limit_bytes=VMEM_LIMIT),
        name="fox_attn",
    )(qp, kp, vtp)


def _mix_ffn_kernel(x_ref, at_ref, u_ref, uprev_ref, wpool_ref, pscale_ref, wout_ref, g2_ref,
                    wg_ref, wu_ref, wd_ref, gf_ref, y_ref, ext_ref, stage_ref, *, final_norm):
    tm = x_ref.shape[1]
    j = pl.program_id(1)

    n_groups = len(POOL_WINDOWS)
    out_chunk = D_MODEL // n_groups

    pad, top = POOL_PAD, POOL_PAD + POOL_HALO
    u = u_ref[0]
    ext_ref[:pad, :] = jnp.zeros((pad, POOL_WIDTH), jnp.float32)
    ext_ref[pad:top, :] = jnp.where(j > 0, uprev_ref[0], 0.0)
    ext_ref[top:, :] = u
    stage_ref[:, :pad, :] = jnp.zeros((stage_ref.shape[0], pad, POOL_GROUP), jnp.float32)

    first_rows = lax.broadcasted_iota(jnp.int32, (POOL_HALO, POOL_GROUP), 0) + 1
    mixed, x1_cols = [], []
    for gi, w in enumerate(POOL_WINDOWS):
        ocols = slice(gi * out_chunk, (gi + 1) * out_chunk)
        x1_cols.append(x_ref[0, :, ocols]
                       + lax.dot_general(at_ref[0], wout_ref[:ATTN_WIDTH, ocols], _TN,
                                         preferred_element_type=jnp.float32))
        cols = slice(gi * POOL_GROUP, (gi + 1) * POOL_GROUP)
        end = ext_ref.shape[0]
        wsum = ext_ref[pad:end, cols] + ext_ref[pad - 1:end - 1, cols]
        span = 2
        while span < w:
            level = stage_ref.at[(span.bit_length() - 2) % stage_ref.shape[0]]
            level[pad:end, :] = wsum
            wsum = wsum + level[pad - span:end - span, :]
            span *= 2
        wsum = wsum[POOL_HALO:, :]
        mean = wsum * (1.0 / w)
        exact_head = wsum[:POOL_HALO, :] / jnp.minimum(first_rows, w).astype(jnp.float32)
        head = jnp.where(j == 0, exact_head, mean[:POOL_HALO, :])
        mean = jnp.concatenate([head, mean[POOL_HALO:, :]], axis=0)
        pooled = (mean - u[:, cols]).astype(jnp.bfloat16)
        mg = jnp.dot(pooled, wpool_ref[gi], preferred_element_type=jnp.float32)
        mixed.append((mg * pscale_ref[:, cols]).astype(jnp.bfloat16))
    pool = jnp.concatenate(mixed, axis=1)
    x1 = jnp.concatenate(x1_cols, axis=1) + jnp.dot(pool, wout_ref[ATTN_WIDTH:, :],
                                                    preferred_element_type=jnp.float32)

    part = tm // 2
    for r0 in (0, part):
        x1r = x1[r0:r0 + part, :]
        h2 = _rms(x1r, g2_ref[...]).astype(jnp.bfloat16)
        ffn = jnp.zeros((part, D_MODEL), jnp.float32)
        for c0 in range(0, D_FF, FF_CHUNK):
            gate = jnp.dot(h2, wg_ref[:, c0:c0 + FF_CHUNK], preferred_element_type=jnp.float32)
            up = jnp.dot(h2, wu_ref[:, c0:c0 + FF_CHUNK], preferred_element_type=jnp.float32)
            act = (gate / (1.0 + jnp.exp(-gate)) * up).astype(jnp.bfloat16)
            ffn = ffn + jnp.dot(act, wd_ref[c0:c0 + FF_CHUNK, :], preferred_element_type=jnp.float32)
        x2 = x1r + ffn
        y_ref[0, r0:r0 + part, :] = _rms(x2, gf_ref[...]) if final_norm else x2


def _mix_ffn(x, attn_t, u, wpool, pscale, wout, g2, wg, wu, wd, gf, final_norm):
    B, S, D = x.shape
    tm = TOKEN_TILE
    halo_blocks = tm // POOL_HALO
    const = lambda shape: pl.BlockSpec(shape, lambda b, j: (0,) * len(shape),
                                       pipeline_mode=pl.Buffered(1))
    return pl.pallas_call(
        functools.partial(_mix_ffn_kernel, final_norm=final_norm),
        grid=(B, S // tm),
        in_specs=[
            pl.BlockSpec((1, tm, D), lambda b, j: (b, j, 0)),
            pl.BlockSpec((1, ATTN_WIDTH, tm), lambda b, j: (b, 0, j)),
            pl.BlockSpec((1, tm, POOL_WIDTH), lambda b, j: (b, j, 0)),
            pl.BlockSpec((1, POOL_HALO, POOL_WIDTH),
                         lambda b, j: (b, jnp.maximum(j * halo_blocks - 1, 0), 0)),
            const(wpool.shape), const(pscale.shape), const(wout.shape), const(g2.shape),
            const(wg.shape), const(wu.shape), const(wd.shape), const(gf.shape),
        ],
        out_specs=pl.BlockSpec((1, tm, D), lambda b, j: (b, j, 0)),
        out_shape=jax.ShapeDtypeStruct((B, S, D), jnp.float32),
        scratch_shapes=[pltpu.VMEM((POOL_PAD + POOL_HALO + tm, POOL_WIDTH), jnp.float32),
                        pltpu.VMEM((2, POOL_PAD + POOL_HALO + tm, POOL_GROUP), jnp.float32)],
        compiler_params=pltpu.CompilerParams(
            dimension_semantics=("arbitrary", "arbitrary"), vmem_limit_bytes=VMEM_LIMIT),
        name="mix_ffn",
    )(x, attn_t, u, u, wpool, pscale, wout, g2, wg, wu, wd, gf)


def kernel(x, norm1_g, w_in, b_forget, w_pool, pool_scale, w_out, norm2_g, w_gate, w_up, w_down, final_g):
    depth = w_in.shape[0]
    bf16 = jnp.bfloat16
    a0 = ATTN_WIDTH
    n_bias = N_BIAS_TERMS * N_HEADS
    lane = np.arange(LANES)
    mod3 = jnp.asarray(np.where(lane < n_bias, lane % N_BIAS_TERMS, N_BIAS_TERMS)[None, :], jnp.int32)
    for layer in range(depth):
        w = w_in[layer]
        wf = jnp.repeat(w[:, 3 * a0:3 * a0 + N_HEADS], N_BIAS_TERMS, axis=1)
        w_all = jnp.concatenate([w[:, :3 * a0], w[:, 3 * a0 + N_HEADS:], wf,
                                 jnp.zeros((D_MODEL, LANES - n_bias), w.dtype)], axis=1).astype(bf16)
        bf = jnp.concatenate([jnp.repeat(b_forget[layer], N_BIAS_TERMS),
                              jnp.zeros((LANES - n_bias,), b_forget.dtype)]).astype(jnp.float32)[None, :]
        (qp, kp, vtp, u), (wg, wu, wd, wo, wp) = _in_proj(
            x, norm1_g[layer][None, :], w_all, bf, mod3,
            (w_gate[layer], w_up[layer], w_down[layer], w_out[layer],
             w_pool[layer].reshape(POOL_WIDTH, POOL_GROUP)))
        attn_t = _fox_attn(qp, kp, vtp)
        x = _mix_ffn(x, attn_t, u, wp.reshape(w_pool.shape[1:]), pool_scale[layer][None, :],
                     wo, norm2_g[layer][None, :], wg, wu, wd,
                     final_g[None, :], final_norm=layer == depth - 1)
    return x
```
